```python
import jax, jax.numpy as jnp
from jax import lax
import numpy as np

D_MODEL = 1024
BATCH = 8
SEQ = 2048
DEPTH = 2

GRID_W = 64
WIN_H = 8
WIN_W = 16
NA_HEADS = 8
NA_HEAD_DIM = 64
NA_WIDTH = NA_HEADS * NA_HEAD_DIM
RET_HEADS = 4
RET_HEAD_DIM = 128
RET_WIDTH = RET_HEADS * RET_HEAD_DIM
RET_CHUNK = 128
RET_THETA_BASE = 10000.0
LRU_WIDTH = 512
LRU_BLOCKS = 8
LRU_BLOCK_DIM = LRU_WIDTH // LRU_BLOCKS
LRU_C = 8.0
CONV_W = 4
CONV_PAD_LEFT = 2
MIN_RAD = 0.9
MAX_RAD = 0.999
N_BRANCH = 3
BRANCH_WIDTH = 512
D_FF = 4 * D_MODEL
PLE_DIM = 256
RMS_EPS = 1e-6
OFF_RET = 3 * NA_WIDTH
OFF_LRU = OFF_RET + 4 * RET_WIDTH
OFF_GATE = OFF_LRU + 2 * LRU_WIDTH
IN_COLS = OFF_GATE + N_BRANCH * D_MODEL

kernel_name = "hybrid_natten_retnet_rglru_encoder"


def rmsnorm(x, g):
    xf = x.astype(jnp.float32)
    y = xf * lax.rsqrt(jnp.mean(xf * xf, axis=-1, keepdims=True) + RMS_EPS)
    return (y * g.astype(jnp.float32)).astype(x.dtype)


def neighborhood_attention(q, k, v, rpb):
    b, s, h, dh = q.shape
    rows = s // GRID_W
    kh = min(WIN_H, rows)
    qg = q.reshape(b, rows, GRID_W, h, dh) * (dh ** -0.5)
    kg = k.reshape(b, rows, GRID_W, h, dh)
    vg = v.reshape(b, rows, GRID_W, h, dh)
    col = jnp.arange(GRID_W)
    col_start = jnp.clip(col - WIN_W // 2, 0, GRID_W - WIN_W)
    col_idx = col_start[:, None] + jnp.arange(WIN_W)[None, :]
    col_bias_idx = col_idx - col[:, None] + (WIN_W - 1)

    def row_block(r):
        r_start = jnp.clip(r - kh // 2, 0, rows - kh)
        k_rows = lax.dynamic_slice_in_dim(kg, r_start, kh, axis=1)
        v_rows = lax.dynamic_slice_in_dim(vg, r_start, kh, axis=1)
        k_win = k_rows[:, :, col_idx]
        v_win = v_rows[:, :, col_idx]
        q_row = lax.dynamic_index_in_dim(qg, r, axis=1, keepdims=False)
        row_bias_idx = r_start + jnp.arange(kh) - r + (WIN_H - 1)
        bias = rpb[:, row_bias_idx[None, :, None], col_bias_idx[:, None, :]]
        scores = jnp.einsum('bchd,bicjhd->bhcij', q_row, k_win).astype(jnp.float32)
        scores = scores + bias.astype(jnp.float32)
        probs = jax.nn.softmax(scores.reshape(b, h, GRID_W, kh * WIN_W), axis=-1)
        probs = probs.reshape(b, h, GRID_W, kh, WIN_W).astype(v.dtype)
        return jnp.einsum('bhcij,bicjhd->bchd', probs, v_win)

    out = lax.map(row_block, jnp.arange(rows))
    return out.transpose(1, 0, 2, 3, 4).reshape(b, s, h * dh)


def rotate_pairs(t, cos, sin):
    t1 = t[..., 0::2]
    t2 = t[..., 1::2]
    return jnp.stack([t1 * cos - t2 * sin, t1 * sin + t2 * cos], axis=-1).reshape(t.shape)


def chunkwise_retention(q, k, v, log_gamma, include_diag):
    b, s, h, dk = q.shape
    dv = v.shape[-1]
    n_chunks = s // RET_CHUNK
    idx = jnp.arange(RET_CHUNK, dtype=jnp.float32)
    diff = idx[:, None] - idx[None, :]
    valid = (diff >= 0) if include_diag else (diff > 0)
    intra = jnp.where(valid[None], jnp.exp(log_gamma[:, None, None] * jnp.maximum(diff, 0.0)[None]), 0.0)
    q_dec = jnp.exp(log_gamma[:, None] * (idx + 1.0)[None])[:, :, None]
    k_dec = jnp.exp(log_gamma[:, None] * (RET_CHUNK - 1.0 - idx)[None])[:, :, None]
    c_dec = jnp.exp(log_gamma * RET_CHUNK)[:, None, None]

    def to_chunks(t):
        return t.reshape(b, n_chunks, RET_CHUNK, h, t.shape[-1]).transpose(1, 0, 3, 2, 4)

    def step(state, inp):
        qi, ki, vi = inp
        scores = jnp.einsum('bhid,bhjd->bhij', qi, ki) * intra
        out = jnp.einsum('bhij,bhjv->bhiv', scores, vi) + jnp.einsum('bhid,bhdv->bhiv', qi * q_dec, state)
        state = state * c_dec + jnp.einsum('bhjd,bhjv->bhdv', ki * k_dec, vi)
        return state, out

    state0 = jnp.zeros((b, h, dk, dv), jnp.float32)
    _, out = lax.scan(step, state0, (to_chunks(q), to_chunks(k), to_chunks(v)))
    return out.transpose(1, 0, 3, 2, 4).reshape(b, s, h, dv)


def retention_branch(q, k, v, g, gn):
    b, s, _ = q.shape
    f = jnp.float32
    q = q.astype(f).reshape(b, s, RET_HEADS, RET_HEAD_DIM)
    k = k.astype(f).reshape(b, s, RET_HEADS, RET_HEAD_DIM)
    v = v.astype(f).reshape(b, s, RET_HEADS, RET_HEAD_DIM)
    pos = jnp.arange(s, dtype=f)
    theta = 1.0 / (RET_THETA_BASE ** jnp.linspace(0.0, 1.0, RET_HEAD_DIM // 2, dtype=f))
    ang = pos[:, None] * theta[None, :]
    cos = jnp.cos(ang)[None, :, None, :]
    sin = jnp.sin(ang)[None, :, None, :]
    q = rotate_pairs(q, cos, sin) * (RET_HEAD_DIM ** -0.5)
    k = rotate_pairs(k, cos, sin)
    hidx = jnp.arange(RET_HEADS, dtype=f)
    log_gamma_fwd = jnp.log1p(-jnp.exp2(-5.0 - hidx))
    log_gamma_bwd = jnp.log1p(-jnp.exp2(-5.5 - hidx))
    y_fwd = chunkwise_retention(q, k, v, log_gamma_fwd, True)
    y_bwd = jnp.flip(chunkwise_retention(jnp.flip(q, 1), jnp.flip(k, 1), jnp.flip(v, 1), log_gamma_bwd, False), 1)
    y = y_fwd + y_bwd
    y = y * lax.rsqrt(jnp.mean(y * y, axis=-1, keepdims=True) + RMS_EPS)
    y = y.reshape(b, s, RET_WIDTH) * gn.astype(f)
    return jax.nn.silu(g.astype(f)) * y


def block_diag_linear(t, w, bias):
    tb = t.reshape(t.shape[0], t.shape[1], LRU_BLOCKS, LRU_BLOCK_DIM)
    return jnp.einsum('bsni,nij->bsnj', tb, w).reshape(t.shape) + bias


def linear_recurrence_combine(c1, c2):
    a1, b1 = c1
    a2, b2 = c2
    return a1 * a2, a2 * b1 + b2


def rg_lru(xs, w_a, b_a, w_x, b_x, lam, reverse):
    f = jnp.float32
    s = xs.shape[1]
    r = jax.nn.sigmoid(block_diag_linear(xs, w_a.astype(f), b_a.astype(f)))
    i = jax.nn.sigmoid(block_diag_linear(xs, w_x.astype(f), b_x.astype(f)))
    log_a = -LRU_C * r * jax.nn.softplus(-lam.astype(f))
    a = jnp.exp(log_a)
    mult = jnp.sqrt(-jnp.expm1(2.0 * log_a))
    first = s - 1 if reverse else 0
    mult = jnp.where((jnp.arange(s) == first)[None, :, None], 1.0, mult)
    _, hs = lax.associative_scan(linear_recurrence_combine, (a, mult * i * xs), axis=1, reverse=reverse)
    return hs


def rglru_branch(xc, gc, conv_w, conv_b, wa, ba, wx, bx, lam):
    f = jnp.float32
    xf = xc.astype(f)
    xf = lax.conv_general_dilated(
        xf, conv_w.astype(f)[:, None, :], window_strides=(1,),
        padding=[(CONV_PAD_LEFT, CONV_W - 1 - CONV_PAD_LEFT)],
        dimension_numbers=('NWC', 'WIO', 'NWC'), feature_group_count=LRU_WIDTH) + conv_b.astype(f)
    y = rg_lru(xf, wa[0], ba[0], wx[0], bx[0], lam[0], False) + rg_lru(xf, wa[1], ba[1], wx[1], bx[1], lam[1], True)
    return jax.nn.gelu(gc.astype(f)) * y


def hybrid_layer(x, p_i, g_mix, w_in, b_gate, na_rpb, ret_gn, conv_w, conv_b,
                 lru_wa, lru_ba, lru_wx, lru_bx, lru_lambda, w_branch, w_out,
                 g_mlp, w_up, w_down, g_ple, w_ple_gate, w_ple):
    b, s, _ = x.shape
    h = rmsnorm(x, g_mix)
    proj = jnp.einsum('bsd,dc->bsc', h, w_in)
    qa, ka, va = jnp.split(proj[..., :OFF_RET], 3, axis=-1)
    qr, kr, vr, gr = jnp.split(proj[..., OFF_RET:OFF_LRU], 4, axis=-1)
    xc, gc = jnp.split(proj[..., OFF_LRU:OFF_GATE], 2, axis=-1)
    gate_logits = proj[..., OFF_GATE:].reshape(b, s, N_BRANCH, D_MODEL)

    def heads(t):
        return t.reshape(b, s, NA_HEADS, NA_HEAD_DIM)

    y_na = neighborhood_attention(heads(qa), heads(ka), heads(va), na_rpb)
    y_ret = retention_branch(qr, kr, vr, gr, ret_gn)
    y_lru = rglru_branch(xc, gc, conv_w, conv_b, lru_wa, lru_ba, lru_wx, lru_bx, lru_lambda)
    ys = jnp.stack([y_na.astype(h.dtype), y_ret.astype(h.dtype), y_lru.astype(h.dtype)], axis=2)
    branch = jnp.einsum('bsnw,nwd->bsnd', ys, w_branch).astype(jnp.float32)
    gates = jax.nn.sigmoid((gate_logits + b_gate).astype(jnp.float32))
    merged = jnp.sum(gates * branch, axis=2).astype(x.dtype)
    x = x + merged @ w_out

    h = rmsnorm(x, g_mlp)
    x = x + jnp.square(jax.nn.relu(h @ w_up)) @ w_down

    h = rmsnorm(x, g_ple)
    x = x + jax.nn.sigmoid(h @ w_ple_gate) * (p_i @ w_ple)
    return x


def setup_inputs(seed: int = 0) -> dict:
    key = jax.random.key(seed)
    ks = jax.random.split(key, 24)
    f = jnp.float32

    def nrm(k, shape, scale):
        return jax.random.normal(k, shape, f) * scale

    u = jax.random.uniform(ks[13], (DEPTH, 2, LRU_WIDTH), f, minval=MIN_RAD, maxval=MAX_RAD)
    sig = u ** (1.0 / LRU_C)
    lam = jnp.log(sig) - jnp.log1p(-sig)
    return {
        'x': nrm(ks[0], (BATCH, SEQ, D_MODEL), 1.0),
        'p': nrm(ks[1], (DEPTH, BATCH, SEQ, PLE_DIM), 1.0),
        'g_mix': 1.0 + nrm(ks[2], (DEPTH, D_MODEL), 0.02),
        'w_in': nrm(ks[3], (DEPTH, D_MODEL, IN_COLS), D_MODEL ** -0.5),
        'b_gate': nrm(ks[4], (DEPTH, N_BRANCH, D_MODEL), 0.02),
        'na_rpb': nrm(ks[5], (DEPTH, NA_HEADS, 2 * WIN_H - 1, 2 * WIN_W - 1), 0.02),
        'ret_gn': 1.0 + nrm(ks[6], (DEPTH, RET_WIDTH), 0.02),
        'conv_w': nrm(ks[7], (DEPTH, CONV_W, LRU_WIDTH), CONV_W ** -0.5),
        'conv_b': nrm(ks[8], (DEPTH, LRU_WIDTH), 0.02),
        'lru_wa': nrm(ks[9], (DEPTH, 2, LRU_BLOCKS, LRU_BLOCK_DIM, LRU_BLOCK_DIM), LRU_BLOCK_DIM ** -0.5),
        'lru_ba': nrm(ks[10], (DEPTH, 2, LRU_WIDTH), 0.02),
        'lru_wx': nrm(ks[11], (DEPTH, 2, LRU_BLOCKS, LRU_BLOCK_DIM, LRU_BLOCK_DIM), LRU_BLOCK_DIM ** -0.5),
        'lru_bx': nrm(ks[12], (DEPTH, 2, LRU_WIDTH), 0.02),
        'lru_lambda': lam,
        'w_branch': nrm(ks[14], (DEPTH, N_BRANCH, BRANCH_WIDTH, D_MODEL), BRANCH_WIDTH ** -0.5),
        'w_out': nrm(ks[15], (DEPTH, D_MODEL, D_MODEL), D_MODEL ** -0.5),
        'g_mlp': 1.0 + nrm(ks[16], (DEPTH, D_MODEL), 0.02),
        'w_up': nrm(ks[17], (DEPTH, D_MODEL, D_FF), D_MODEL ** -0.5),
        'w_down': nrm(ks[18], (DEPTH, D_FF, D_MODEL), D_FF ** -0.5),
        'g_ple': 1.0 + nrm(ks[19], (DEPTH, D_MODEL), 0.02),
        'w_ple_gate': nrm(ks[20], (DEPTH, D_MODEL, D_MODEL), D_MODEL ** -0.5),
        'w_ple': nrm(ks[21], (DEPTH, PLE_DIM, D_MODEL), PLE_DIM ** -0.5),
        'g_final': 1.0 + nrm(ks[22], (D_MODEL,), 0.02),
    }


def reference(x, p, g_mix, w_in, b_gate, na_rpb, ret_gn, conv_w, conv_b,
              lru_wa, lru_ba, lru_wx, lru_bx, lru_lambda, w_branch, w_out,
              g_mlp, w_up, w_down, g_ple, w_ple_gate, w_ple, g_final):
    for i in range(DEPTH):
        x = hybrid_layer(x, p[i], g_mix[i], w_in[i], b_gate[i], na_rpb[i], ret_gn[i],
                         conv_w[i], conv_b[i], lru_wa[i], lru_ba[i], lru_wx[i], lru_bx[i],
                         lru_lambda[i], w_branch[i], w_out[i], g_mlp[i], w_up[i], w_down[i],
                         g_ple[i], w_ple_gate[i], w_ple[i])
    return rmsnorm(x, g_final)
```

```python
import functools
import math

import numpy as np
import jax
import jax.numpy as jnp
from jax import lax
from jax.experimental import pallas as pl
from jax.experimental.pallas import tpu as pltpu

F32 = jnp.float32
BF16 = jnp.bfloat16

D_MODEL = 1024
BATCH = 8
SEQ = 2048
DEPTH = 2
TOKENS = BATCH * SEQ

GRID_W = 64
GRID_H = SEQ // GRID_W
WIN_H = 8
WIN_W = 16
NA_HEADS = 8
NA_HEAD_DIM = 64
NA_WIDTH = NA_HEADS * NA_HEAD_DIM
RET_HEADS = 4
RET_HEAD_DIM = 128
RET_WIDTH = RET_HEADS * RET_HEAD_DIM
RET_THETA_BASE = 10000.0
LRU_WIDTH = 512
LRU_BLOCKS = 8
LRU_BLOCK_DIM = LRU_WIDTH // LRU_BLOCKS
LRU_C = 8.0
N_BRANCH = 3
D_FF = 4 * D_MODEL
PLE_DIM = 256
RMS_EPS = 1e-6

OFF_RET = 3 * NA_WIDTH
OFF_LRU = OFF_RET + 4 * RET_WIDTH
OFF_GATE = OFF_LRU + 2 * LRU_WIDTH
IN_COLS = OFF_GATE + N_BRANCH * D_MODEL

P_GATE = 0
P_NA = N_BRANCH * D_MODEL
P_RET = P_NA + 3 * NA_WIDTH
P_LRU = P_RET + 4 * RET_WIDTH

LANES = 128
SUBLANES = 8
VMEM_LIMIT = 56 * 1024 * 1024

RET_CHUNK = 256
MASK_NEG = -1e30


def _cparams(sem):
    return pltpu.CompilerParams(dimension_semantics=sem, vmem_limit_bytes=VMEM_LIMIT)


def _rms_scale(x):
    return lax.rsqrt(jnp.mean(x * x, axis=-1, keepdims=True) + RMS_EPS)


IN_TM = 1024
IN_TN = 1536


def _inproj_kernel(x_ref, g_ref, w_ref, o_ref, h_ref):
    @pl.when(pl.program_id(1) == 0)
    def _():
        x = x_ref[...]
        h_ref[...] = (x * _rms_scale(x) * g_ref[...]).astype(BF16)

    o_ref[...] = jnp.dot(h_ref[...], w_ref[...], preferred_element_type=F32).astype(o_ref.dtype)


def _in_proj(x, g, w):
    return pl.pallas_call(
        _inproj_kernel,
        grid=(TOKENS // IN_TM, IN_COLS // IN_TN),
        in_specs=[
            pl.BlockSpec((IN_TM, D_MODEL), lambda i, j: (i, 0)),
            pl.BlockSpec((1, D_MODEL), lambda i, j: (0, 0)),
            pl.BlockSpec((D_MODEL, IN_TN), lambda i, j: (0, j)),
        ],
        out_specs=pl.BlockSpec((IN_TM, IN_TN), lambda i, j: (i, j)),
        out_shape=jax.ShapeDtypeStruct((TOKENS, IN_COLS), BF16),
        scratch_shapes=[pltpu.VMEM((IN_TM, D_MODEL), BF16)],
        compiler_params=_cparams(("parallel", "arbitrary")),
        name="in_proj",
    )(x, g, w)


NA_PAIRS = NA_HEADS // 2
NA_KEYS = WIN_H * GRID_W
NA_BIAS_ROWS = 2 * WIN_H - 2


def _na_kernel(q_ref, k_ref, v_ref, t2_ref, o_ref):
    lane = lax.broadcasted_iota(jnp.int32, (GRID_W, LANES), 1)
    lo = lane < NA_HEAD_DIM

    def row(r, carry):
        r_start = jnp.clip(r - WIN_H // 2, 0, GRID_H - WIN_H)
        d = r_start - r + (WIN_H - 1)
        q_r = q_ref[pl.ds(pl.multiple_of(r * GRID_W, GRID_W), GRID_W), :]
        kw0 = pl.multiple_of(r_start * GRID_W, GRID_W)
        outs = []
        for p in range(NA_PAIRS):
            cols = slice(p * LANES, (p + 1) * LANES)
            qp = q_r[:, cols]
            zero = jnp.zeros_like(qp)
            q2 = jnp.concatenate([jnp.where(lo, qp, zero), jnp.where(lo, zero, qp)], axis=0)
            kp = k_ref[pl.ds(kw0, NA_KEYS), cols]
            vp = v_ref[pl.ds(kw0, NA_KEYS), cols]
            s = lax.dot_general(q2, kp, (((1,), (1,)), ((), ())), preferred_element_type=F32)
            s = s * (NA_HEAD_DIM ** -0.5)
            s = jnp.concatenate(
                [s[:, c * LANES:(c + 1) * LANES] + t2_ref[p, d + 2 * c] for c in range(NA_KEYS // LANES)],
                axis=1)
            m = jnp.max(s, axis=-1, keepdims=True)
            e = jnp.exp(s - m)
            l = jnp.sum(e, axis=-1, keepdims=True)
            o = jnp.dot(e.astype(BF16), vp, preferred_element_type=F32) / l
            outs.append(jnp.where(lo, o[:GRID_W], o[GRID_W:]))
        o_ref[pl.ds(pl.multiple_of(r * GRID_W, GRID_W), GRID_W), :] = (
            jnp.concatenate(outs, axis=1).astype(o_ref.dtype))
        return carry

    lax.fori_loop(0, GRID_H, row, 0)


def _na_bias_table(rpb):
    col = np.arange(GRID_W)
    col_start = np.clip(col - WIN_W // 2, 0, GRID_W - WIN_W)
    kc = np.arange(GRID_W)[None, :]
    valid = (kc >= col_start[:, None]) & (kc < col_start[:, None] + WIN_W)
    cidx = np.clip(kc - col[:, None] + (WIN_W - 1), 0, 2 * WIN_W - 2)
    t = jnp.where(jnp.asarray(valid)[None, None], rpb.astype(F32)[:, :, cidx], MASK_NEG)
    t2 = jnp.concatenate([t[:, :NA_BIAS_ROWS], t[:, 1:NA_BIAS_ROWS + 1]], axis=-1)
    t2 = t2.reshape(NA_PAIRS, 2, NA_BIAS_ROWS, GRID_W, LANES).transpose(0, 2, 1, 3, 4)
    return t2.reshape(NA_PAIRS, NA_BIAS_ROWS, 2 * GRID_W, LANES)


def _na(proj, t2):
    blk = P_NA // NA_WIDTH
    return pl.pallas_call(
        _na_kernel,
        grid=(BATCH,),
        in_specs=[
            pl.BlockSpec((SEQ, NA_WIDTH), lambda b: (b, blk)),
            pl.BlockSpec((SEQ, NA_WIDTH), lambda b: (b, blk + 1)),
            pl.BlockSpec((SEQ, NA_WIDTH), lambda b: (b, blk + 2)),
            pl.BlockSpec((NA_PAIRS, NA_BIAS_ROWS, 2 * GRID_W, LANES), lambda b: (0, 0, 0, 0)),
        ],
        out_specs=pl.BlockSpec((SEQ, NA_WIDTH), lambda b: (b, 0)),
        out_shape=jax.ShapeDtypeStruct((TOKENS, NA_WIDTH), BF16),
        compiler_params=_cparams(("parallel",)),
        name="na",
    )(proj, proj, proj, t2)


RET_NCHUNK = SEQ // RET_CHUNK


def _ret_kernel(q_ref, k_ref, v_ref, g_ref, cos_ref, sin_ref, dmat_ref, dec_ref, cdec_ref, gn_ref,
                o_ref, qr_ref, kr_ref, y_ref):
    lane = lax.broadcasted_iota(jnp.int32, (SEQ, LANES), 1)
    even = (lane & 1) == 0
    cos = cos_ref[...]
    sin = sin_ref[...]

    def rotate(t):
        partner = jnp.where(even, pltpu.roll(t, LANES - 1, 1), pltpu.roll(t, 1, 1))
        return t * cos + partner * sin

    qr_ref[...] = rotate(q_ref[...].astype(F32)) * (RET_HEAD_DIM ** -0.5)
    kr_ref[...] = rotate(k_ref[...].astype(F32))

    dmat = dmat_ref[0]
    q_f, k_f, q_b, k_b = dec_ref[0, 0], dec_ref[0, 1], dec_ref[0, 2], dec_ref[0, 3]
    c_f, c_b = cdec_ref[0, 0:1, :], cdec_ref[0, 1:2, :]

    def kv_update(state, cdec, kc, kdec, vc):
        kv = lax.dot_general((kc * kdec).astype(BF16), vc, (((0,), (0,)), ((), ())),
                             preferred_element_type=F32)
        return state * cdec + kv

    state = jnp.zeros((RET_HEAD_DIM, RET_HEAD_DIM), F32)
    for c in range(RET_NCHUNK):
        rows = slice(c * RET_CHUNK, (c + 1) * RET_CHUNK)
        qc, kc, vc = qr_ref[rows, :], kr_ref[rows, :], v_ref[rows, :]
        s = lax.dot_general(qc.astype(BF16), kc.astype(BF16), (((1,), (1,)), ((), ())),
                            preferred_element_type=F32) * dmat
        y = jnp.dot(s.astype(BF16), vc, preferred_element_type=F32)
        if c > 0:
            y = y + jnp.dot((qc * q_f).astype(BF16), state.astype(BF16), preferred_element_type=F32)
        if c < RET_NCHUNK - 1:
            state = kv_update(state, c_f, kc, k_f, vc)
        y_ref[rows, :] = y

    state = jnp.zeros((RET_HEAD_DIM, RET_HEAD_DIM), F32)
    for c in range(RET_NCHUNK - 1, -1, -1):
        rows = slice(c * RET_CHUNK, (c + 1) * RET_CHUNK)
        qc, kc, vc = qr_ref[rows, :], kr_ref[rows, :], v_ref[rows, :]
        if c < RET_NCHUNK - 1:
            y_ref[rows, :] += jnp.dot((qc * q_b).astype(BF16), state.astype(BF16),
                                      preferred_element_type=F32)
        if c > 0:
            state = kv_update(state, c_b, kc, k_b, vc)

    y = y_ref[...]
    y = y * _rms_scale(y) * gn_ref[...]
    g = g_ref[...].astype(F32)
    o_ref[...] = (g * jax.nn.sigmoid(g) * y).astype(o_ref.dtype)


def _ret_tables():
    pos = jnp.arange(SEQ, dtype=F32)
    theta = 1.0 / (RET_THETA_BASE ** jnp.linspace(0.0, 1.0, RET_HEAD_DIM // 2, dtype=F32))
    ang = pos[:, None] * theta[None, :]
    sign = jnp.asarray(np.tile(np.array([-1.0, 1.0], np.float32), RET_HEAD_DIM // 2))
    cos = jnp.repeat(jnp.cos(ang), 2, axis=1)
    sin = jnp.repeat(jnp.sin(ang), 2, axis=1) * sign
    hidx = jnp.arange(RET_HEADS, dtype=F32)
    lg_f = jnp.log1p(-jnp.exp2(-5.0 - hidx))[:, None]
    lg_b = jnp.log1p(-jnp.exp2(-5.5 - hidx))[:, None]
    idx = jnp.arange(RET_CHUNK, dtype=F32)
    diff = idx[:, None] - idx[None, :]
    dmat = jnp.where(diff >= 0, jnp.exp(lg_f[:, :, None] * jnp.maximum(diff, 0.0)),
                     jnp.exp(lg_b[:, :, None] * jnp.maximum(-diff, 0.0)))
    dec = jnp.stack([jnp.exp(lg_f * (idx + 1.0)), jnp.exp(lg_f * (RET_CHUNK - 1.0 - idx)),
                     jnp.exp(lg_b * (RET_CHUNK - idx)), jnp.exp(lg_b * idx)], axis=1)
    dec = jnp.broadcast_to(dec[..., None], (RET_HEADS, 4, RET_CHUNK, LANES))
    cdec = jnp.stack([jnp.exp(lg_f * RET_CHUNK), jnp.exp(lg_b * RET_CHUNK)], axis=1)
    cdec = jnp.broadcast_to(cdec, (RET_HEADS, 2, LANES))
    return cos, sin, dmat, dec, cdec


def _retention(proj, gn, tables):
    cos, sin, dmat, dec, cdec = tables
    blk = P_RET // RET_HEAD_DIM
    tok = lambda off: pl.BlockSpec((SEQ, RET_HEAD_DIM), lambda b, h: (b, blk + off * RET_HEADS + h))
    const2 = pl.BlockSpec((SEQ, RET_HEAD_DIM), lambda b, h: (0, 0))
    return pl.pallas_call(
        _ret_kernel,
        grid=(BATCH, RET_HEADS),
        in_specs=[
            tok(0), tok(1), tok(2), tok(3), const2, const2,
            pl.BlockSpec((1, RET_CHUNK, RET_CHUNK), lambda b, h: (h, 0, 0)),
            pl.BlockSpec((1, 4, RET_CHUNK, LANES), lambda b, h: (h, 0, 0, 0)),
            pl.BlockSpec((1, 2, LANES), lambda b, h: (h, 0, 0)),
            pl.BlockSpec((1, RET_HEAD_DIM), lambda b, h: (0, h)),
        ],
        out_specs=pl.BlockSpec((SEQ, RET_HEAD_DIM), lambda b, h: (b, h)),
        out_shape=jax.ShapeDtypeStruct((TOKENS, RET_WIDTH), BF16),
        scratch_shapes=[pltpu.VMEM((SEQ, RET_HEAD_DIM), F32), pltpu.VMEM((SEQ, RET_HEAD_DIM), F32),
                        pltpu.VMEM((SEQ, RET_HEAD_DIM), F32)],
        compiler_params=_cparams(("parallel", "arbitrary")),
        name="retention",
    )(proj, proj, proj, proj, cos, sin, dmat, dec, cdec, gn)


LRU_ROWS = 256
LRU_PAD = SUBLANES
LRU_SLABS = LRU_WIDTH // LANES
LRU_GROUPS = SEQ // SUBLANES


def _softplus(x):
    return jnp.maximum(x, 0.0) + jnp.log1p(jnp.exp(-jnp.abs(x)))


def _lru_kernel(xc_ref, gc_ref, cw_ref, cb_ref, w_ref, bias_ref, lam_ref, o_ref,
                xpad_ref, a_ref, b_ref):
    zeros = jnp.zeros((LRU_PAD, LRU_WIDTH), F32)
    xpad_ref[0:LRU_PAD, :] = zeros
    xpad_ref[LRU_PAD + SEQ:, :] = zeros
    xpad_ref[LRU_PAD:LRU_PAD + SEQ, :] = xc_ref[...].astype(F32)

    cw = cw_ref[...]
    cb = cb_ref[...]
    neg_c_sp = -LRU_C * _softplus(-lam_ref[...])
    rows = lax.broadcasted_iota(jnp.int32, (LRU_ROWS, LANES), 0)

    def gates(ci, carry):
        t0 = pl.multiple_of(ci * LRU_ROWS, LRU_ROWS)
        xw = xpad_ref[pl.ds(t0, LRU_ROWS + 2 * LRU_PAD), :]
        taps = [xw[LRU_PAD - 2 + j:LRU_PAD - 2 + j + LRU_ROWS, :] for j in range(4)]
        xf = taps[0] * cw[0:1] + taps[1] * cw[1:2] + taps[2] * cw[2:3] + taps[3] * cw[3:4] + cb
        t = rows + t0
        for s in range(LRU_SLABS):
            cols = slice(s * LANES, (s + 1) * LANES)
            xs = xf[:, cols]
            z = jnp.dot(xs.astype(BF16), w_ref[s], preferred_element_type=F32)
            for dr in range(2):
                r = jax.nn.sigmoid(z[:, (2 * dr) * LANES:(2 * dr + 1) * LANES] + bias_ref[2 * dr:2 * dr + 1, cols])
                i = jax.nn.sigmoid(z[:, (2 * dr + 1) * LANES:(2 * dr + 2) * LANES]
                                   + bias_ref[2 * dr + 1:2 * dr + 2, cols])
                log_a = r * neg_c_sp[dr:dr + 1, cols]
                a = jnp.exp(log_a)
                mult = jnp.sqrt(1.0 - jnp.exp(2.0 * log_a))
                first = 0 if dr == 0 else SEQ - 1
                mult = jnp.where(t == first, 1.0, mult)
                a_ref[dr, pl.ds(t0, LRU_ROWS), cols] = a
                b_ref[dr, pl.ds(t0, LRU_ROWS), cols] = mult * i * xs
        return carry

    lax.fori_loop(0, SEQ // LRU_ROWS, gates, 0)

    sub = lax.broadcasted_iota(jnp.int32, (SUBLANES, LRU_WIDTH), 0)

    def local_scan(a, b, reverse):
        for sh in (1, 2, 4):
            if reverse:
                keep = sub < SUBLANES - sh
                a_n = pltpu.roll(a, SUBLANES - sh, 0)
                b_n = pltpu.roll(b, SUBLANES - sh, 0)
            else:
                keep = sub >= sh
                a_n = pltpu.roll(a, sh, 0)
                b_n = pltpu.roll(b, sh, 0)
            b = jnp.where(keep, a * b_n, 0.0) + b
            a = jnp.where(keep, a * a_n, a)
        return a, b

    def scan(j, carry):
        hf, hb = carry
        tf = pl.multiple_of(j * SUBLANES, SUBLANES)
        tb = pl.multiple_of((LRU_GROUPS - 1 - j) * SUBLANES, SUBLANES)
        a, b = local_scan(a_ref[0, pl.ds(tf, SUBLANES), :], b_ref[0, pl.ds(tf, SUBLANES), :], False)
        h = b + a * hf
        b_ref[0, pl.ds(tf, SUBLANES), :] = h
        hf = jnp.broadcast_to(h[SUBLANES - 1:SUBLANES, :], h.shape)
        a, b = local_scan(a_ref[1, pl.ds(tb, SUBLANES), :], b_ref[1, pl.ds(tb, SUBLANES), :], True)
        h = b + a * hb
        b_ref[1, pl.ds(tb, SUBLANES), :] = h
        hb = jnp.broadcast_to(h[0:1, :], h.shape)
        return hf, hb

    h0 = jnp.zeros((SUBLANES, LRU_WIDTH), F32)
    lax.fori_loop(0, LRU_GROUPS, scan, (h0, h0))

    def finish(ci, carry):
        t0 = pl.multiple_of(ci * LRU_ROWS, LRU_ROWS)
        y = b_ref[0, pl.ds(t0, LRU_ROWS), :] + b_ref[1, pl.ds(t0, LRU_ROWS), :]
        g = gc_ref[pl.ds(t0, LRU_ROWS), :].astype(F32)
        o_ref[pl.ds(t0, LRU_ROWS), :] = (jax.nn.gelu(g, approximate=True) * y).astype(o_ref.dtype)
        return carry

    lax.fori_loop(0, SEQ // LRU_ROWS, finish, 0)


def _lru_weights(wa, wx):
    def slab(w):
        w = w.reshape(LRU_SLABS, 2, LRU_BLOCK_DIM, LRU_BLOCK_DIM)
        z = jnp.zeros_like(w[:, 0])
        top = jnp.concatenate([w[:, 0], z], axis=-1)
        bot = jnp.concatenate([z, w[:, 1]], axis=-1)
        return jnp.concatenate([top, bot], axis=-2)
    return jnp.concatenate([slab(wa[0]), slab(wx[0]), slab(wa[1]), slab(wx[1])], axis=-1).astype(BF16)


def _rglru(proj, conv_w, conv_b, w4, bias4, lam):
    blk = P_LRU // LRU_WIDTH
    full = lambda shape: pl.BlockSpec(shape, lambda b: (0,) * len(shape))
    return pl.pallas_call(
        _lru_kernel,
        grid=(BATCH,),
        in_specs=[
            pl.BlockSpec((SEQ, LRU_WIDTH), lambda b: (b, blk)),
            pl.BlockSpec((SEQ, LRU_WIDTH), lambda b: (b, blk + 1)),
            full((4, LRU_WIDTH)), full((1, LRU_WIDTH)),
            full((LRU_SLABS, LANES, 4 * LANES)), full((4, LRU_WIDTH)), full((2, LRU_WIDTH)),
        ],
        out_specs=pl.BlockSpec((SEQ, LRU_WIDTH), lambda b: (b, 0)),
        out_shape=jax.ShapeDtypeStruct((TOKENS, LRU_WIDTH), BF16),
        scratch_shapes=[pltpu.VMEM((SEQ + 2 * LRU_PAD, LRU_WIDTH), F32),
                        pltpu.VMEM((2, SEQ, LRU_WIDTH), F32), pltpu.VMEM((2, SEQ, LRU_WIDTH), F32)],
        compiler_params=_cparams(("parallel",)),
        name="rglru",
    )(proj, proj, conv_w, conv_b, w4, bias4, lam)


MERGE_TM = 512


def _merge_kernel(x_ref, yn_ref, yr_ref, yl_ref, g0_ref, g1_ref, g2_ref, bg_ref, wb_ref, wo_ref, o_ref):
    merged = None
    for n, (y_ref, gl_ref) in enumerate(((yn_ref, g0_ref), (yr_ref, g1_ref), (yl_ref, g2_ref))):
        branch = jnp.dot(y_ref[...], wb_ref[n], preferred_element_type=F32)
        gate = jax.nn.sigmoid(gl_ref[...].astype(F32) + bg_ref[n:n + 1, :])
        merged = gate * branch if merged is None else merged + gate * branch
    o_ref[...] = x_ref[...] + jnp.dot(merged.astype(BF16), wo_ref[...], preferred_element_type=F32)


def _merge(x, y_na, y_ret, y_lru, proj, b_gate, w_branch, w_out):
    ytile = pl.BlockSpec((MERGE_TM, NA_WIDTH), lambda i: (i, 0))
    gate = lambda n: pl.BlockSpec((MERGE_TM, D_MODEL), lambda i: (i, P_GATE // D_MODEL + n))
    xtile = pl.BlockSpec((MERGE_TM, D_MODEL), lambda i: (i, 0))
    return pl.pallas_call(
        _merge_kernel,
        grid=(TOKENS // MERGE_TM,),
        in_specs=[
            xtile, ytile, ytile, ytile, gate(0), gate(1), gate(2),
            pl.BlockSpec((N_BRANCH, D_MODEL), lambda i: (0, 0)),
            pl.BlockSpec((N_BRANCH, NA_WIDTH, D_MODEL), lambda i: (0, 0, 0)),
            pl.BlockSpec((D_MODEL, D_MODEL), lambda i: (0, 0)),
        ],
        out_specs=xtile,
        out_shape=jax.ShapeDtypeStruct((TOKENS, D_MODEL), F32),
        compiler_params=_cparams(("parallel",)),
        name="merge",
    )(x, y_na, y_ret, y_lru, proj, proj, proj, b_gate, w_branch, w_out)


MLP_TM = 1024
MLP_TF = 1024


def _mlp_kernel(x_ref, g_ref, wu_ref, wd_ref, o_ref, h_ref):
    @pl.when(pl.program_id(1) == 0)
    def _():
        x = x_ref[...]
        h_ref[...] = (x * _rms_scale(x) * g_ref[...]).astype(BF16)
        o_ref[...] = x

    u = jnp.maximum(jnp.dot(h_ref[...], wu_ref[...], preferred_element_type=F32), 0.0)
    o_ref[...] += jnp.dot((u * u).astype(BF16), wd_ref[...], preferred_element_type=F32)


def _mlp(x, g, w_up, w_down):
    xtile = pl.BlockSpec((MLP_TM, D_MODEL), lambda i, k: (i, 0))
    return pl.pallas_call(
        _mlp_kernel,
        grid=(TOKENS // MLP_TM, D_FF // MLP_TF),
        in_specs=[
            xtile,
            pl.BlockSpec((1, D_MODEL), lambda i, k: (0, 0)),
            pl.BlockSpec((D_MODEL, MLP_TF), lambda i, k: (0, k)),
            pl.BlockSpec((MLP_TF, D_MODEL), lambda i, k: (k, 0)),
        ],
        out_specs=xtile,
        out_shape=jax.ShapeDtypeStruct((TOKENS, D_MODEL), F32),
        scratch_shapes=[pltpu.VMEM((MLP_TM, D_MODEL), BF16)],
        compiler_params=_cparams(("parallel", "arbitrary")),
        name="mlp",
    )(x, g, w_up, w_down)


PLE_TM = 1024


def _ple_kernel(x_ref, p_ref, g_ref, wg_ref, wp_ref, gf_ref, o_ref, *, final):
    x = x_ref[...]
    h = (x * _rms_scale(x) * g_ref[...]).astype(BF16)
    gate = jax.nn.sigmoid(jnp.dot(h, wg_ref[...], preferred_element_type=F32))
    emb = jnp.dot(p_ref[...].astype(BF16), wp_ref[...], preferred_element_type=F32)
    x = x + gate * emb
    if final:
        x = x * _rms_scale(x) * gf_ref[...]
    o_ref[...] = x


def _ple(x, p, g, w_gate, w_ple, g_final, final):
    xtile = pl.BlockSpec((PLE_TM, D_MODEL), lambda i: (i, 0))
    vec = pl.BlockSpec((1, D_MODEL), lambda i: (0, 0))
    return pl.pallas_call(
        functools.partial(_ple_kernel, final=final),
        grid=(TOKENS // PLE_TM,),
        in_specs=[
            xtile,
            pl.BlockSpec((PLE_TM, PLE_DIM), lambda i: (i, 0)),
            vec,
            pl.BlockSpec((D_MODEL, D_MODEL), lambda i: (0, 0)),
            pl.BlockSpec((PLE_DIM, D_MODEL), lambda i: (0, 0)),
            vec,
        ],
        out_specs=xtile,
        out_shape=jax.ShapeDtypeStruct((TOKENS, D_MODEL), F32),
        compiler_params=_cparams(("parallel",)),
        name="ple_final" if final else "ple",
    )(x, p, g, w_gate, w_ple, g_final)


def _permute_w_in(w):
    return jnp.concatenate([w[:, OFF_GATE:], w[:, :OFF_GATE]], axis=1).astype(BF16)


def kernel(x, p, g_mix, w_in, b_gate, na_rpb, ret_gn, conv_w, conv_b, lru_wa, lru_ba, lru_wx, lru_bx,
           lru_lambda, w_branch, w_out, g_mlp, w_up, w_down, g_ple, w_ple_gate, w_ple, g_final):
    ret_tables = _ret_tables()
    xt = x.reshape(TOKENS, D_MODEL)
    row = lambda v: v.reshape(1, -1).astype(F32)
    for i in range(DEPTH):
        proj = _in_proj(xt, row(g_mix[i]), _permute_w_in(w_in[i]))
        y_na = _na(proj, _na_bias_table(na_rpb[i]))
        y_ret = _retention(proj, row(ret_gn[i]), ret_tables)
        bias4 = jnp.stack([lru_ba[i, 0], lru_bx[i, 0], lru_ba[i, 1], lru_bx[i, 1]]).astype(F32)
        y_lru = _rglru(proj, conv_w[i].astype(F32), row(conv_b[i]), _lru_weights(lru_wa[i], lru_wx[i]),
                       bias4, lru_lambda[i].astype(F32))
        xt = _merge(xt, y_na, y_ret, y_lru, proj, b_gate[i].astype(F32), w_branch[i].astype(BF16),
                    w_out[i].astype(BF16))
        xt = _mlp(xt, row(g_mlp[i]), w_up[i].astype(BF16), w_down[i].astype(BF16))
        xt = _ple(xt, p[i].reshape(TOKENS, PLE_DIM), row(g_ple[i]), w_ple_gate[i].astype(BF16),
                  w_ple[i].astype(BF16), row(g_final), i == DEPTH - 1)
    return xt.reshape(BATCH, SEQ, D_MODEL)
```

```python
import functools
import math

import numpy as np
import jax
import jax.numpy as jnp
from jax import lax
from jax.experimental import pallas as pl
from jax.experimental.pallas import tpu as pltpu

F32 = jnp.float32
BF16 = jnp.bfloat16

D_MODEL = 1024
BATCH = 8
SEQ = 2048
DEPTH = 2
TOKENS = BATCH * SEQ

GRID_W = 64
GRID_H = SEQ // GRID_W
WIN_H = 8
WIN_W = 16
NA_HEADS = 8
NA_HEAD_DIM = 64
NA_WIDTH = NA_HEADS * NA_HEAD_DIM
RET_HEADS = 4
RET_HEAD_DIM = 128
RET_WIDTH = RET_HEADS * RET_HEAD_DIM
RET_THETA_BASE = 10000.0
LRU_WIDTH = 512
LRU_BLOCKS = 8
LRU_BLOCK_DIM = LRU_WIDTH // LRU_BLOCKS
LRU_C = 8.0
N_BRANCH = 3
D_FF = 4 * D_MODEL
PLE_DIM = 256
RMS_EPS = 1e-6

P_NA = 0
P_RET = 3 * NA_WIDTH
P_LRU = P_RET + 4 * RET_WIDTH
P_GATE = P_LRU + 2 * LRU_WIDTH
IN_COLS = P_GATE + N_BRANCH * D_MODEL

LANES = 128
SUBLANES = 8
VMEM_LIMIT = 56 * 1024 * 1024

RET_CHUNK = 256
MASK_NEG = -1e30
LOG2E = math.log2(math.e)


def _cparams(sem):
    return pltpu.CompilerParams(dimension_semantics=sem, vmem_limit_bytes=VMEM_LIMIT)


def _rms_scale(x):
    return lax.rsqrt(jnp.mean(x * x, axis=-1, keepdims=True) + RMS_EPS)


def _layer_spec(shape, index_map):
    return pl.BlockSpec((None,) + shape, index_map)


IN_TM = 1024
IN_TN = 1536


def _inproj_kernel(x_ref, g_ref, w_ref, o_ref, h_ref):
    @pl.when(pl.program_id(1) == 0)
    def _():
        x = x_ref[...]
        h_ref[...] = (x * _rms_scale(x) * g_ref[...]).astype(BF16)

    o_ref[...] = jnp.dot(h_ref[...], w_ref[...], preferred_element_type=F32).astype(o_ref.dtype)


def _in_proj(x, g, w, layer):
    return pl.pallas_call(
        _inproj_kernel,
        grid=(TOKENS // IN_TM, IN_COLS // IN_TN),
        in_specs=[
            pl.BlockSpec((IN_TM, D_MODEL), lambda i, j: (i, 0)),
            _layer_spec((1, D_MODEL), lambda i, j: (layer, 0, 0)),
            _layer_spec((D_MODEL, IN_TN), lambda i, j: (layer, 0, j)),
        ],
        out_specs=pl.BlockSpec((IN_TM, IN_TN), lambda i, j: (i, j)),
        out_shape=jax.ShapeDtypeStruct((TOKENS, IN_COLS), BF16),
        scratch_shapes=[pltpu.VMEM((IN_TM, D_MODEL), BF16)],
        compiler_params=_cparams(("parallel", "arbitrary")),
        name="in_proj",
    )(x, g, w)


NA_PAIRS = NA_HEADS // 2
NA_KEYS = WIN_H * GRID_W
NA_BIAS_ROWS = 2 * WIN_H - 2
NA_QSCALE = NA_HEAD_DIM ** -0.5 * LOG2E


def _na_kernel(q_ref, k_ref, v_ref, t2_ref, o_ref, s_ref):
    lane = lax.broadcasted_iota(jnp.int32, (GRID_W, LANES), 1)
    lo = lane < NA_HEAD_DIM

    def window(r):
        if isinstance(r, int):
            r_start = min(max(r - WIN_H // 2, 0), GRID_H - WIN_H)
            return r_start * GRID_W, r_start - r + (WIN_H - 1)
        r_start = jnp.clip(r - WIN_H // 2, 0, GRID_H - WIN_H)
        return pl.multiple_of(r_start * GRID_W, GRID_W), r_start - r + (WIN_H - 1)

    def token_row(r):
        return r * GRID_W if isinstance(r, int) else pl.multiple_of(r * GRID_W, GRID_W)

    def scores(r, slot):
        kw0, d = window(r)
        q_r = q_ref[pl.ds(token_row(r), GRID_W), :]
        for p in range(NA_PAIRS):
            cols = slice(p * LANES, (p + 1) * LANES)
            qp = q_r[:, cols]
            zero = jnp.zeros_like(qp)
            q2 = jnp.concatenate([jnp.where(lo, qp, zero), jnp.where(lo, zero, qp)], axis=0)
            kp = k_ref[pl.ds(kw0, NA_KEYS), cols]
            s = lax.dot_general(q2, kp, (((1,), (1,)), ((), ())), preferred_element_type=F32)
            for c in range(NA_KEYS // LANES):
                kc = slice(c * LANES, (c + 1) * LANES)
                s_ref[slot, p, :, kc] = s[:, kc] + t2_ref[p, d + 2 * c]

    def attend(r, slot):
        kw0, _ = window(r)
        outs = []
        for p in range(NA_PAIRS):
            cols = slice(p * LANES, (p + 1) * LANES)
            s = s_ref[slot, p]
            e = jnp.exp2(s - jnp.max(s, axis=-1, keepdims=True))
            l = jnp.sum(e, axis=-1, keepdims=True)
            vp = v_ref[pl.ds(kw0, NA_KEYS), cols]
            o = jnp.dot(e.astype(BF16), vp, preferred_element_type=F32) / l
            outs.append(jnp.where(lo, o[:GRID_W], o[GRID_W:]))
        o_ref[pl.ds(token_row(r), GRID_W), :] = (
            jnp.concatenate(outs, axis=1).astype(o_ref.dtype))

    scores(0, 0)

    def two_rows(j, carry):
        r = 2 * j
        scores(r + 1, 1)
        attend(r, 0)
        scores(jnp.minimum(r + 2, GRID_H - 1), 0)
        attend(r + 1, 1)
        return carry

    lax.fori_loop(0, GRID_H // 2, two_rows, 0)


_NA_COL = np.arange(GRID_W)
_NA_COL_START = np.clip(_NA_COL - WIN_W // 2, 0, GRID_W - WIN_W)
_NA_VALID = ((_NA_COL[None, :] >= _NA_COL_START[:, None])
             & (_NA_COL[None, :] < _NA_COL_START[:, None] + WIN_W))


def _na_bias_table(rpb):
    h, nr, nc = rpb.shape
    v = jnp.pad(rpb.astype(F32) * LOG2E, ((0, 0), (0, 0), (GRID_W - WIN_W, LANES - nc - (GRID_W - WIN_W))))
    flat = jnp.broadcast_to(v[:, :, None, :], (h, nr, GRID_W, LANES)).reshape(h, nr, GRID_W * LANES)
    skew = flat[:, :, :GRID_W * (LANES - 1)].reshape(h, nr, GRID_W, LANES - 1)
    t = jnp.where(jnp.asarray(_NA_VALID), skew[..., GRID_W - 1:2 * GRID_W - 1], MASK_NEG)
    t2 = jnp.concatenate([t[:, :NA_BIAS_ROWS], t[:, 1:NA_BIAS_ROWS + 1]], axis=-1)
    t2 = t2.reshape(h // 2, 2, NA_BIAS_ROWS, GRID_W, LANES).transpose(0, 2, 1, 3, 4)
    return t2.reshape(h // 2, NA_BIAS_ROWS, 2 * GRID_W, LANES)


def _na(proj, t2, layer):
    blk = P_NA // NA_WIDTH
    return pl.pallas_call(
        _na_kernel,
        grid=(BATCH,),
        in_specs=[
            pl.BlockSpec((SEQ, NA_WIDTH), lambda b: (b, blk)),
            pl.BlockSpec((SEQ, NA_WIDTH), lambda b: (b, blk + 1)),
            pl.BlockSpec((SEQ, NA_WIDTH), lambda b: (b, blk + 2)),
            _layer_spec((NA_PAIRS, NA_BIAS_ROWS, 2 * GRID_W, LANES), lambda b: (layer, 0, 0, 0, 0)),
        ],
        out_specs=pl.BlockSpec((SEQ, NA_WIDTH), lambda b: (b, 0)),
        out_shape=jax.ShapeDtypeStruct((TOKENS, NA_WIDTH), BF16),
        scratch_shapes=[pltpu.VMEM((2, NA_PAIRS, 2 * GRID_W, NA_KEYS), F32)],
        compiler_params=_cparams(("parallel",)),
        name="na",
    )(proj, proj, proj, t2)


RET_NCHUNK = SEQ // RET_CHUNK


def _ret_kernel(q_ref, k_ref, v_ref, g_ref, cos_ref, sin_ref, dmat_ref, dec_ref, cdec_ref, gn_ref,
                o_ref, qr_ref, kr_ref, y_ref):
    lane = lax.broadcasted_iota(jnp.int32, (SEQ, LANES), 1)
    even = (lane & 1) == 0
    cos = cos_ref[...]
    sin = sin_ref[...]

    def rotate(t):
        partner = jnp.where(even, pltpu.roll(t, LANES - 1, 1), pltpu.roll(t, 1, 1))
        return t * cos + partner * sin

    qr_ref[...] = rotate(q_ref[...].astype(F32)) * (RET_HEAD_DIM ** -0.5)
    kr_ref[...] = rotate(k_ref[...].astype(F32))

    dmat = dmat_ref[0]
    q_f, k_f, q_b, k_b = dec_ref[0, 0], dec_ref[0, 1], dec_ref[0, 2], dec_ref[0, 3]
    c_f, c_b = cdec_ref[0, 0:1, :], cdec_ref[0, 1:2, :]

    def kv_update(state, cdec, kc, kdec, vc):
        kv = lax.dot_general((kc * kdec).astype(BF16), vc, (((0,), (0,)), ((), ())),
                             preferred_element_type=F32)
        return state * cdec + kv

    state = jnp.zeros((RET_HEAD_DIM, RET_HEAD_DIM), F32)
    for c in range(RET_NCHUNK):
        rows = slice(c * RET_CHUNK, (c + 1) * RET_CHUNK)
        qc, kc, vc = qr_ref[rows, :], kr_ref[rows, :], v_ref[rows, :]
        s = lax.dot_general(qc.astype(BF16), kc.astype(BF16), (((1,), (1,)), ((), ())),
                            preferred_element_type=F32) * dmat
        y = jnp.dot(s.astype(BF16), vc, preferred_element_type=F32)
        if c > 0:
            y = y + jnp.dot((qc * q_f).astype(BF16), state.astype(BF16), preferred_element_type=F32)
        if c < RET_NCHUNK - 1:
            state = kv_update(state, c_f, kc, k_f, vc)
        y_ref[rows, :] = y

    state = jnp.zeros((RET_HEAD_DIM, RET_HEAD_DIM), F32)
    for c in range(RET_NCHUNK - 1, -1, -1):
        rows = slice(c * RET_CHUNK, (c + 1) * RET_CHUNK)
        qc, kc, vc = qr_ref[rows, :], kr_ref[rows, :], v_ref[rows, :]
        if c < RET_NCHUNK - 1:
            y_ref[rows, :] += jnp.dot((qc * q_b).astype(BF16), state.astype(BF16),
                                      preferred_element_type=F32)
        if c > 0:
            state = kv_update(state, c_b, kc, k_b, vc)

    y = y_ref[...]
    y = y * _rms_scale(y) * gn_ref[...]
    g = g_ref[...].astype(F32)
    o_ref[...] = (g * jax.nn.sigmoid(g) * y).astype(o_ref.dtype)


def _ret_tables():
    pos = np.arange(SEQ, dtype=np.float64)
    theta = 1.0 / (RET_THETA_BASE ** np.linspace(0.0, 1.0, RET_HEAD_DIM // 2))
    ang = pos[:, None] * theta[None, :]
    sign = np.tile(np.array([-1.0, 1.0]), RET_HEAD_DIM // 2)
    cos = np.repeat(np.cos(ang), 2, axis=1)
    sin = np.repeat(np.sin(ang), 2, axis=1) * sign
    hidx = np.arange(RET_HEADS, dtype=np.float64)
    lg_f = np.log1p(-np.exp2(-5.0 - hidx))[:, None]
    lg_b = np.log1p(-np.exp2(-5.5 - hidx))[:, None]
    idx = np.arange(RET_CHUNK, dtype=np.float64)
    diff = idx[:, None] - idx[None, :]
    dmat = np.where(diff >= 0, np.exp(lg_f[:, :, None] * np.maximum(diff, 0.0)),
                    np.exp(lg_b[:, :, None] * np.maximum(-diff, 0.0)))
    dec = np.stack([np.exp(lg_f * (idx + 1.0)), np.exp(lg_f * (RET_CHUNK - 1.0 - idx)),
                    np.exp(lg_b * (RET_CHUNK - idx)), np.exp(lg_b * idx)], axis=1)
    dec = np.broadcast_to(dec[..., None], (RET_HEADS, 4, RET_CHUNK, LANES))
    cdec = np.stack([np.exp(lg_f * RET_CHUNK), np.exp(lg_b * RET_CHUNK)], axis=1)
    cdec = np.broadcast_to(cdec, (RET_HEADS, 2, LANES))
    return tuple(jnp.asarray(np.ascontiguousarray(t), F32) for t in (cos, sin, dmat, dec, cdec))


def _retention(proj, gn, tables, layer):
    cos, sin, dmat, dec, cdec = tables
    blk = P_RET // RET_HEAD_DIM
    tok = lambda off: pl.BlockSpec((SEQ, RET_HEAD_DIM), lambda b, h: (b, blk + off * RET_HEADS + h))
    const2 = pl.BlockSpec((SEQ, RET_HEAD_DIM), lambda b, h: (0, 0))
    return pl.pallas_call(
        _ret_kernel,
        grid=(BATCH, RET_HEADS),
        in_specs=[
            tok(0), tok(1), tok(2), tok(3), const2, const2,
            pl.BlockSpec((1, RET_CHUNK, RET_CHUNK), lambda b, h: (h, 0, 0)),
            pl.BlockSpec((1, 4, RET_CHUNK, LANES), lambda b, h: (h, 0, 0, 0)),
            pl.BlockSpec((1, 2, LANES), lambda b, h: (h, 0, 0)),
            _layer_spec((1, RET_HEAD_DIM), lambda b, h: (layer, 0, h)),
        ],
        out_specs=pl.BlockSpec((SEQ, RET_HEAD_DIM), lambda b, h: (b, h)),
        out_shape=jax.ShapeDtypeStruct((TOKENS, RET_WIDTH), BF16),
        scratch_shapes=[pltpu.VMEM((SEQ, RET_HEAD_DIM), F32), pltpu.VMEM((SEQ, RET_HEAD_DIM), F32),
                        pltpu.VMEM((SEQ, RET_HEAD_DIM), F32)],
        compiler_params=_cparams(("parallel", "arbitrary")),
        name="retention",
    )(proj, proj, proj, proj, cos, sin, dmat, dec, cdec, gn)


LRU_SEGS = SUBLANES
LRU_SEG = SEQ // LRU_SEGS
LRU_PAD = SUBLANES
LRU_SLABS = LRU_WIDTH // LANES


def _softplus(x):
    return jnp.maximum(x, 0.0) + jnp.log1p(jnp.exp(-jnp.abs(x)))


def _lru_kernel(xc_ref, gc_ref, cw_ref, cb_ref, w_ref, bias_ref, lam_ref, o_ref,
                xpad_ref, a_ref, b_ref, hin_ref):
    zeros = jnp.zeros((LRU_PAD, LRU_WIDTH), F32)
    xpad_ref[0:LRU_PAD, :] = zeros
    xpad_ref[LRU_PAD + SEQ:, :] = zeros
    xpad_ref[LRU_PAD:LRU_PAD + SEQ, :] = xc_ref[...].astype(F32)

    cw = cw_ref[...]
    cb = cb_ref[...]
    half_k = (-0.5 * LRU_C * LOG2E) * _softplus(-lam_ref[...])
    row = lax.broadcasted_iota(jnp.int32, (LRU_SEG, LANES), 0)

    def gates(seg, first_dir):
        t0 = pl.multiple_of(seg * LRU_SEG, LRU_SEG)
        xw = xpad_ref[pl.ds(t0, LRU_SEG + 2 * LRU_PAD), :]
        taps = [xw[LRU_PAD - 2 + j:LRU_PAD - 2 + j + LRU_SEG, :] for j in range(4)]
        xf = taps[0] * cw[0:1] + taps[1] * cw[1:2] + taps[2] * cw[2:3] + taps[3] * cw[3:4] + cb
        for s in range(LRU_SLABS):
            cols = slice(s * LANES, (s + 1) * LANES)
            xs = xf[:, cols]
            xh = 0.5 * xs
            z = jnp.dot(xs.astype(BF16), w_ref[s], preferred_element_type=F32)
            for dr in range(2):
                zr = z[:, (2 * dr) * LANES:(2 * dr + 1) * LANES] + bias_ref[2 * dr:2 * dr + 1, cols]
                zi = z[:, (2 * dr + 1) * LANES:(2 * dr + 2) * LANES] + bias_ref[2 * dr + 1:2 * dr + 2, cols]
                hk = half_k[dr:dr + 1, cols]
                a = jnp.exp2(jnp.tanh(zr) * hk + hk)
                if first_dir == dr:
                    a = jnp.where(row == (0 if dr == 0 else LRU_SEG - 1), 0.0, a)
                om = 1.0 - a * a
                mult = om * lax.rsqrt(jnp.maximum(om, 1e-30))
                a_ref[dr, s, pl.ds(seg, LRU_SEG, stride=LRU_SEGS), :] = a
                b_ref[dr, s, pl.ds(seg, LRU_SEG, stride=LRU_SEGS), :] = mult * (jnp.tanh(zi) * xh + xh)

    gates(0, 0)
    lax.fori_loop(1, LRU_SEGS - 1, lambda seg, c: (gates(seg, None), c)[1], 0)
    gates(LRU_SEGS - 1, 1)

    def scan(j, carry):
        new = []
        for dr in range(2):
            r0 = pl.multiple_of((j if dr == 0 else LRU_SEG - 1 - j) * LRU_SEGS, LRU_SEGS)
            for s in range(LRU_SLABS):
                h, p = carry[dr * LRU_SLABS + s]
                a = a_ref[dr, s, pl.ds(r0, LRU_SEGS), :]
                h = a * h + b_ref[dr, s, pl.ds(r0, LRU_SEGS), :]
                p = a * p
                b_ref[dr, s, pl.ds(r0, LRU_SEGS), :] = h
                a_ref[dr, s, pl.ds(r0, LRU_SEGS), :] = p
                new.append((h, p))
        return tuple(new)

    init = (jnp.zeros((LRU_SEGS, LANES), F32), jnp.ones((LRU_SEGS, LANES), F32))
    ends = lax.fori_loop(0, LRU_SEG, scan, (init,) * (2 * LRU_SLABS), unroll=2)

    for dr in range(2):
        order = range(LRU_SEGS) if dr == 0 else range(LRU_SEGS - 1, -1, -1)
        for s in range(LRU_SLABS):
            h_end, p_end = ends[dr * LRU_SLABS + s]
            carry_in = jnp.zeros((1, LANES), F32)
            rows_in = [None] * LRU_SEGS
            for seg in order:
                rows_in[seg] = carry_in
                carry_in = h_end[seg:seg + 1, :] + p_end[seg:seg + 1, :] * carry_in
            hin_ref[dr, s] = jnp.concatenate(rows_in, axis=0)

    def finish(seg, carry):
        t0 = pl.multiple_of(seg * LRU_SEG, LRU_SEG)
        g = gc_ref[pl.ds(t0, LRU_SEG), :].astype(F32)
        g = jax.nn.gelu(g, approximate=True)
        for s in range(LRU_SLABS):
            cols = slice(s * LANES, (s + 1) * LANES)
            y = None
            for dr in range(2):
                h = b_ref[dr, s, pl.ds(seg, LRU_SEG, stride=LRU_SEGS), :]
                p = a_ref[dr, s, pl.ds(seg, LRU_SEG, stride=LRU_SEGS), :]
                h = h + p * hin_ref[dr, s, pl.ds(seg, 1), :]
                y = h if y is None else y + h
            o_ref[pl.ds(t0, LRU_SEG), cols] = (g[:, cols] * y).astype(o_ref.dtype)
        return carry

    lax.fori_loop(0, LRU_SEGS, finish, 0)


def _lru_weights(wa, wx):
    def slab(w):
        w = 0.5 * w.reshape(DEPTH, LRU_SLABS, 2, LRU_BLOCK_DIM, LRU_BLOCK_DIM)
        z = jnp.zeros_like(w[:, :, 0])
        top = jnp.concatenate([w[:, :, 0], z], axis=-1)
        bot = jnp.concatenate([z, w[:, :, 1]], axis=-1)
        return jnp.concatenate([top, bot], axis=-2)
    return jnp.concatenate([slab(wa[:, 0]), slab(wx[:, 0]), slab(wa[:, 1]), slab(wx[:, 1])],
                           axis=-1).astype(BF16)


def _rglru(proj, conv_w, conv_b, w4, bias4, lam, layer):
    blk = P_LRU // LRU_WIDTH
    full = lambda *shape: _layer_spec(shape, lambda b: (layer,) + (0,) * len(shape))
    state = pltpu.VMEM((2, LRU_SLABS, SEQ, LANES), F32)
    return pl.pallas_call(
        _lru_kernel,
        grid=(BATCH,),
        in_specs=[
            pl.BlockSpec((SEQ, LRU_WIDTH), lambda b: (b, blk)),
            pl.BlockSpec((SEQ, LRU_WIDTH), lambda b: (b, blk + 1)),
            full(4, LRU_WIDTH), full(1, LRU_WIDTH),
            full(LRU_SLABS, LANES, 4 * LANES), full(4, LRU_WIDTH), full(2, LRU_WIDTH),
        ],
        out_specs=pl.BlockSpec((SEQ, LRU_WIDTH), lambda b: (b, 0)),
        out_shape=jax.ShapeDtypeStruct((TOKENS, LRU_WIDTH), BF16),
        scratch_shapes=[pltpu.VMEM((SEQ + 2 * LRU_PAD, LRU_WIDTH), F32), state, state,
                        pltpu.VMEM((2, LRU_SLABS, LRU_SEGS, LANES), F32)],
        compiler_params=_cparams(("parallel",)),
        name="rglru",
    )(proj, proj, conv_w, conv_b, w4, bias4, lam)


MERGE_TM = 512
GATE_HALF = D_MODEL // 2


def _merge_kernel(x_ref, yn_ref, yr_ref, yl_ref, g00, g01, g10, g11, g20, g21, bg_ref, wb_ref, wo_ref, o_ref):
    merged = None
    branches = ((yn_ref, g00, g01), (yr_ref, g10, g11), (yl_ref, g20, g21))
    for n, (y_ref, ga_ref, gb_ref) in enumerate(branches):
        branch = jnp.dot(y_ref[...], wb_ref[n], preferred_element_type=F32)
        logits = jnp.concatenate([ga_ref[...], gb_ref[...]], axis=1).astype(F32)
        gate = jax.nn.sigmoid(logits + bg_ref[n:n + 1, :])
        merged = gate * branch if merged is None else merged + gate * branch
    o_ref[...] = x_ref[...] + jnp.dot(merged.astype(BF16), wo_ref[...], preferred_element_type=F32)


def _merge(x, y_na, y_ret, y_lru, proj, b_gate, w_branch, w_out, layer):
    ytile = pl.BlockSpec((MERGE_TM, NA_WIDTH), lambda i: (i, 0))
    gate = lambda c: pl.BlockSpec((MERGE_TM, GATE_HALF), lambda i: (i, P_GATE // GATE_HALF + c))
    xtile = pl.BlockSpec((MERGE_TM, D_MODEL), lambda i: (i, 0))
    return pl.pallas_call(
        _merge_kernel,
        grid=(TOKENS // MERGE_TM,),
        in_specs=[
            xtile, ytile, ytile, ytile, gate(0), gate(1), gate(2), gate(3), gate(4), gate(5),
            _layer_spec((N_BRANCH, D_MODEL), lambda i: (layer, 0, 0)),
            _layer_spec((N_BRANCH, NA_WIDTH, D_MODEL), lambda i: (layer, 0, 0, 0)),
            _layer_spec((D_MODEL, D_MODEL), lambda i: (layer, 0, 0)),
        ],
        out_specs=xtile,
        out_shape=jax.ShapeDtypeStruct((TOKENS, D_MODEL), F32),
        compiler_params=_cparams(("parallel",)),
        name="merge",
    )(x, y_na, y_ret, y_lru, proj, proj, proj, proj, proj, proj, b_gate, w_branch, w_out)


MLP_TM = 1024
MLP_TF = 1024


def _mlp_kernel(x_ref, g_ref, wu_ref, wd_ref, o_ref, h_ref):
    @pl.when(pl.program_id(1) == 0)
    def _():
        x = x_ref[...]
        h_ref[...] = (x * _rms_scale(x) * g_ref[...]).astype(BF16)
        o_ref[...] = x

    u = jnp.maximum(jnp.dot(h_ref[...], wu_ref[...], preferred_element_type=F32), 0.0)
    o_ref[...] += jnp.dot((u * u).astype(BF16), wd_ref[...], preferred_element_type=F32)


def _mlp(x, g, w_up, w_down, layer):
    xtile = pl.BlockSpec((MLP_TM, D_MODEL), lambda i, k: (i, 0))
    return pl.pallas_call(
        _mlp_kernel,
        grid=(TOKENS // MLP_TM, D_FF // MLP_TF),
        in_specs=[
            xtile,
            _layer_spec((1, D_MODEL), lambda i, k: (layer, 0, 0)),
            _layer_spec((D_MODEL, MLP_TF), lambda i, k: (layer, 0, k)),
            _layer_spec((MLP_TF, D_MODEL), lambda i, k: (layer, k, 0)),
        ],
        out_specs=xtile,
        out_shape=jax.ShapeDtypeStruct((TOKENS, D_MODEL), F32),
        scratch_shapes=[pltpu.VMEM((MLP_TM, D_MODEL), BF16)],
        compiler_params=_cparams(("parallel", "arbitrary")),
        name="mlp",
    )(x, g, w_up, w_down)


PLE_TM = 1024


def _ple_kernel(x_ref, p_ref, g_ref, wg_ref, wp_ref, gf_ref, o_ref, *, final):
    x = x_ref[...]
    h = (x * _rms_scale(x) * g_ref[...]).astype(BF16)
    gate = jax.nn.sigmoid(jnp.dot(h, wg_ref[...], preferred_element_type=F32))
    emb = jnp.dot(p_ref[...].astype(BF16), wp_ref[...], preferred_element_type=F32)
    x = x + gate * emb
    if final:
        x = x * _rms_scale(x) * gf_ref[...]
    o_ref[...] = x


def _ple(x, p, g, w_gate, w_ple, g_final, layer):
    xtile = pl.BlockSpec((PLE_TM, D_MODEL), lambda i: (i, 0))
    final = layer == DEPTH - 1
    return pl.pallas_call(
        functools.partial(_ple_kernel, final=final),
        grid=(TOKENS // PLE_TM,),
        in_specs=[
            xtile,
            _layer_spec((PLE_TM, PLE_DIM), lambda i: (layer, i, 0)),
            _layer_spec((1, D_MODEL), lambda i: (layer, 0, 0)),
            _layer_spec((D_MODEL, D_MODEL), lambda i: (layer, 0, 0)),
            _layer_spec((PLE_DIM, D_MODEL), lambda i: (layer, 0, 0)),
            pl.BlockSpec((1, D_MODEL), lambda i: (0, 0)),
        ],
        out_specs=xtile,
        out_shape=jax.ShapeDtypeStruct((TOKENS, D_MODEL), F32),
        compiler_params=_cparams(("parallel",)),
        name="ple_final" if final else "ple",
    )(x, p, g, w_gate, w_ple, g_final)


_IN_COL_SCALE = np.ones((IN_COLS,), np.float32)
_IN_COL_SCALE[P_NA:P_NA + NA_WIDTH] = NA_QSCALE


def kernel(x, p, g_mix, w_in, b_gate, na_rpb, ret_gn, conv_w, conv_b, lru_wa, lru_ba, lru_wx, lru_bx,
           lru_lambda, w_branch, w_out, g_mlp, w_up, w_down, g_ple, w_ple_gate, w_ple, g_final):
    rows = lambda v: v.reshape(DEPTH, 1, -1).astype(F32)
    w_in_b = (w_in * jnp.asarray(_IN_COL_SCALE)).astype(BF16)
    w_branch_b, w_out_b = w_branch.astype(BF16), w_out.astype(BF16)
    w_up_b, w_down_b = w_up.astype(BF16), w_down.astype(BF16)
    w_gate_b, w_ple_b = w_ple_gate.astype(BF16), w_ple.astype(BF16)
    g_mix_r, g_mlp_r, g_ple_r, gn_r, conv_b_r = rows(g_mix), rows(g_mlp), rows(g_ple), rows(ret_gn), rows(conv_b)
    t2 = _na_bias_table(na_rpb.reshape(DEPTH * NA_HEADS, 2 * WIN_H - 1, 2 * WIN_W - 1)).reshape(
        DEPTH, NA_PAIRS, NA_BIAS_ROWS, 2 * GRID_W, LANES)
    lru_w4 = _lru_weights(lru_wa, lru_wx)
    lru_bias4 = 0.5 * jnp.stack([lru_ba[:, 0], lru_bx[:, 0], lru_ba[:, 1], lru_bx[:, 1]], axis=1).astype(F32)
    ret_tables = _ret_tables()
    pt = p.reshape(DEPTH, TOKENS, PLE_DIM)
    g_final_r = g_final.reshape(1, D_MODEL).astype(F32)

    xt = x.reshape(TOKENS, D_MODEL)
    for i in range(DEPTH):
        proj = _in_proj(xt, g_mix_r, w_in_b, i)
        y_na = _na(proj, t2, i)
        y_ret = _retention(proj, gn_r, ret_tables, i)
        y_lru = _rglru(proj, conv_w.astype(F32), conv_b_r, lru_w4, lru_bias4, lru_lambda.astype(F32), i)
        xt = _merge(xt, y_na, y_ret, y_lru, proj, b_gate.astype(F32), w_branch_b, w_out_b, i)
        xt = _mlp(xt, g_mlp_r, w_up_b, w_down_b, i)
        xt = _ple(xt, pt, g_ple_r, w_gate_b, w_ple_b, g_final_r, i)
    return xt.reshape(BATCH, SEQ, D_MODEL)
```

```python
import functools
import math

import numpy as np
import jax
import jax.numpy as jnp
from jax import lax
from jax.experimental import pallas as pl
from jax.experimental.pallas import tpu as pltpu

F32 = jnp.float32
BF16 = jnp.bfloat16

D_MODEL = 1024
BATCH = 8
SEQ = 2048
DEPTH = 2
TOKENS = BATCH * SEQ

GRID_W = 64
GRID_H = SEQ // GRID_W
WIN_H = 8
WIN_W = 16
NA_HEADS = 8
NA_HEAD_DIM = 64
NA_WIDTH = NA_HEADS * NA_HEAD_DIM
RET_HEADS = 4
RET_HEAD_DIM = 128
RET_WIDTH = RET_HEADS * RET_HEAD_DIM
RET_THETA_BASE = 10000.0
LRU_WIDTH = 512
LRU_BLOCKS = 8
LRU_BLOCK_DIM = LRU_WIDTH // LRU_BLOCKS
LRU_C = 8.0
N_BRANCH = 3
D_FF = 4 * D_MODEL
PLE_DIM = 256
RMS_EPS = 1e-6

P_NA = 0
P_RET = 3 * NA_WIDTH
P_LRU = P_RET + 4 * RET_WIDTH
P_GATE = P_LRU + 2 * LRU_WIDTH
IN_COLS = P_GATE + N_BRANCH * D_MODEL

LANES = 128
SUBLANES = 8
VMEM_LIMIT = 56 * 1024 * 1024

RET_CHUNK = 256
MASK_NEG = -1e30
LOG2E = math.log2(math.e)


def _cparams(sem):
    return pltpu.CompilerParams(dimension_semantics=sem, vmem_limit_bytes=VMEM_LIMIT)


def _rms_scale(x):
    return lax.rsqrt(jnp.mean(x * x, axis=-1, keepdims=True) + RMS_EPS)


def _layer_spec(shape, index_map):
    return pl.BlockSpec((None,) + shape, index_map)


IN_TM = 1024
IN_TN = 2560


def _inproj_kernel(x_ref, g_ref, w_ref, o_ref, h_ref):
    @pl.when(pl.program_id(1) == 0)
    def _():
        x = x_ref[...]
        h_ref[...] = (x * _rms_scale(x) * g_ref[...]).astype(BF16)

    o_ref[...] = jnp.dot(h_ref[...], w_ref[...], preferred_element_type=F32).astype(o_ref.dtype)


def _in_proj(x, g, w, layer):
    return pl.pallas_call(
        _inproj_kernel,
        grid=(TOKENS // IN_TM, IN_COLS // IN_TN),
        in_specs=[
            pl.BlockSpec((IN_TM, D_MODEL), lambda i, j: (i, 0)),
            _layer_spec((1, D_MODEL), lambda i, j: (layer, 0, 0)),
            _layer_spec((D_MODEL, IN_TN), lambda i, j: (layer, 0, j)),
        ],
        out_specs=pl.BlockSpec((IN_TM, IN_TN), lambda i, j: (i, j)),
        out_shape=jax.ShapeDtypeStruct((TOKENS, IN_COLS), BF16),
        scratch_shapes=[pltpu.VMEM((IN_TM, D_MODEL), BF16)],
        compiler_params=_cparams(("parallel", "arbitrary")),
        name="in_proj",
    )(x, g, w)


NA_PAIRS = NA_HEADS // 2
NA_KEYS = WIN_H * GRID_W
NA_BIAS_ROWS = 2 * WIN_H - 2
NA_QSCALE = NA_HEAD_DIM ** -0.5 * LOG2E


def _na_kernel(q_ref, k_ref, v_ref, rpb_ref, o_ref, s_ref, t2_ref):
    lane = lax.broadcasted_iota(jnp.int32, (GRID_W, LANES), 1)
    lo = lane < NA_HEAD_DIM

    @pl.when(pl.program_id(0) == 0)
    def _():
        qcol = lax.broadcasted_iota(jnp.int32, (GRID_W, LANES), 0)
        kcol = jnp.where(lo, lane, lane - GRID_W)
        col_start = jnp.clip(qcol - WIN_W // 2, 0, GRID_W - WIN_W)
        off = kcol - col_start

        def build(ridx, carry):
            for h in range(NA_HEADS):
                xa = jnp.broadcast_to(rpb_ref[h, pl.ds(ridx, 1), :], (GRID_W, LANES))
                xb = jnp.broadcast_to(rpb_ref[h, pl.ds(ridx + 1, 1), :], (GRID_W, LANES))
                ta = pltpu.roll(xa, GRID_W + 1, 1, stride=1, stride_axis=0)
                tb = pltpu.roll(xb, 1, 1, stride=1, stride_axis=0)
                tile = jnp.where(off >= 0, jnp.where(off < WIN_W, jnp.where(lo, ta, tb), MASK_NEG), MASK_NEG)
                t2_ref[h // 2, ridx, (h % 2) * GRID_W:(h % 2 + 1) * GRID_W, :] = tile
            return carry

        lax.fori_loop(0, NA_BIAS_ROWS, build, 0)

    def window(r):
        if isinstance(r, int):
            r_start = min(max(r - WIN_H // 2, 0), GRID_H - WIN_H)
            return r_start * GRID_W, r_start - r + (WIN_H - 1)
        r_start = jnp.clip(r - WIN_H // 2, 0, GRID_H - WIN_H)
        return pl.multiple_of(r_start * GRID_W, GRID_W), r_start - r + (WIN_H - 1)

    def token_row(r):
        return r * GRID_W if isinstance(r, int) else pl.multiple_of(r * GRID_W, GRID_W)

    def scores(r, slot):
        kw0, d = window(r)
        q_r = q_ref[pl.ds(token_row(r), GRID_W), :]
        for p in range(NA_PAIRS):
            cols = slice(p * LANES, (p + 1) * LANES)
            qp = q_r[:, cols]
            zero = jnp.zeros_like(qp)
            q2 = jnp.concatenate([jnp.where(lo, qp, zero), jnp.where(lo, zero, qp)], axis=0)
            kp = k_ref[pl.ds(kw0, NA_KEYS), cols]
            s = lax.dot_general(q2, kp, (((1,), (1,)), ((), ())), preferred_element_type=F32)
            for c in range(NA_KEYS // LANES):
                kc = slice(c * LANES, (c + 1) * LANES)
                s_ref[slot, p, :, kc] = s[:, kc] + t2_ref[p, d + 2 * c]

    def attend(r, slot):
        kw0, _ = window(r)
        outs = []
        for p in range(NA_PAIRS):
            cols = slice(p * LANES, (p + 1) * LANES)
            s = s_ref[slot, p]
            e = jnp.exp2(s - jnp.max(s, axis=-1, keepdims=True))
            l = jnp.sum(e, axis=-1, keepdims=True)
            vp = v_ref[pl.ds(kw0, NA_KEYS), cols]
            o = jnp.dot(e.astype(BF16), vp, preferred_element_type=F32) / l
            outs.append(jnp.where(lo, o[:GRID_W], o[GRID_W:]))
        o_ref[pl.ds(token_row(r), GRID_W), :] = (
            jnp.concatenate(outs, axis=1).astype(o_ref.dtype))

    scores(0, 0)

    def two_rows(j, carry):
        r = 2 * j
        scores(r + 1, 1)
        attend(r, 0)
        scores(jnp.minimum(r + 2, GRID_H - 1), 0)
        attend(r + 1, 1)
        return carry

    lax.fori_loop(0, GRID_H // 2, two_rows, 0)


NA_RPB_ROWS = 2 * WIN_H
NA_RPB_LANE0 = GRID_W - WIN_W


def _na_pad_rpb(rpb):
    pad = ((0, 0), (0, 0), (0, NA_RPB_ROWS - rpb.shape[2]),
           (NA_RPB_LANE0, LANES - rpb.shape[3] - NA_RPB_LANE0))
    return jnp.pad(rpb.astype(F32) * LOG2E, pad)


def _na(proj, rpb, layer):
    blk = P_NA // NA_WIDTH
    return pl.pallas_call(
        _na_kernel,
        grid=(BATCH,),
        in_specs=[
            pl.BlockSpec((SEQ, NA_WIDTH), lambda b: (b, blk)),
            pl.BlockSpec((SEQ, NA_WIDTH), lambda b: (b, blk + 1)),
            pl.BlockSpec((SEQ, NA_WIDTH), lambda b: (b, blk + 2)),
            _layer_spec((NA_HEADS, NA_RPB_ROWS, LANES), lambda b: (layer, 0, 0, 0)),
        ],
        out_specs=pl.BlockSpec((SEQ, NA_WIDTH), lambda b: (b, 0)),
        out_shape=jax.ShapeDtypeStruct((TOKENS, NA_WIDTH), BF16),
        scratch_shapes=[pltpu.VMEM((2, NA_PAIRS, 2 * GRID_W, NA_KEYS), F32),
                        pltpu.VMEM((NA_PAIRS, NA_BIAS_ROWS, 2 * GRID_W, LANES), F32)],
        compiler_params=_cparams(("arbitrary",)),
        name="na",
    )(proj, proj, proj, rpb)


RET_NCHUNK = SEQ // RET_CHUNK


def _ret_kernel(q_ref, k_ref, v_ref, g_ref, cos_ref, sin_ref, dmat_ref, dec_ref, cdec_ref, gn_ref,
                o_ref, qr_ref, kr_ref, y_ref):
    lane = lax.broadcasted_iota(jnp.int32, (SEQ, LANES), 1)
    even = (lane & 1) == 0
    cos = cos_ref[...]
    sin = sin_ref[...]

    def rotate(t):
        partner = jnp.where(even, pltpu.roll(t, LANES - 1, 1), pltpu.roll(t, 1, 1))
        return t * cos + partner * sin

    qr_ref[...] = rotate(q_ref[...].astype(F32)) * (RET_HEAD_DIM ** -0.5)
    kr_ref[...] = rotate(k_ref[...].astype(F32))

    dmat = dmat_ref[0]
    q_f, k_f, q_b, k_b = dec_ref[0, 0], dec_ref[0, 1], dec_ref[0, 2], dec_ref[0, 3]
    c_f, c_b = cdec_ref[0, 0:1, :], cdec_ref[0, 1:2, :]

    def kv_update(state, cdec, kc, kdec, vc):
        kv = lax.dot_general((kc * kdec).astype(BF16), vc, (((0,), (0,)), ((), ())),
                             preferred_element_type=F32)
        return state * cdec + kv

    state = jnp.zeros((RET_HEAD_DIM, RET_HEAD_DIM), F32)
    for c in range(RET_NCHUNK):
        rows = slice(c * RET_CHUNK, (c + 1) * RET_CHUNK)
        qc, kc, vc = qr_ref[rows, :], kr_ref[rows, :], v_ref[rows, :]
        s = lax.dot_general(qc.astype(BF16), kc.astype(BF16), (((1,), (1,)), ((), ())),
                            preferred_element_type=F32) * dmat
        y = jnp.dot(s.astype(BF16), vc, preferred_element_type=F32)
        if c > 0:
            y = y + jnp.dot((qc * q_f).astype(BF16), state.astype(BF16), preferred_element_type=F32)
        if c < RET_NCHUNK - 1:
            state = kv_update(state, c_f, kc, k_f, vc)
        y_ref[rows, :] = y

    state = jnp.zeros((RET_HEAD_DIM, RET_HEAD_DIM), F32)
    for c in range(RET_NCHUNK - 1, -1, -1):
        rows = slice(c * RET_CHUNK, (c + 1) * RET_CHUNK)
        qc, kc, vc = qr_ref[rows, :], kr_ref[rows, :], v_ref[rows, :]
        if c < RET_NCHUNK - 1:
            y_ref[rows, :] += jnp.dot((qc * q_b).astype(BF16), state.astype(BF16),
                                      preferred_element_type=F32)
        if c > 0:
            state = kv_update(state, c_b, kc, k_b, vc)

    y = y_ref[...]
    y = y * _rms_scale(y) * gn_ref[...]
    g = g_ref[...].astype(F32)
    o_ref[...] = (g * jax.nn.sigmoid(g) * y).astype(o_ref.dtype)


def _ret_tables():
    pos = np.arange(SEQ, dtype=np.float64)
    theta = 1.0 / (RET_THETA_BASE ** np.linspace(0.0, 1.0, RET_HEAD_DIM // 2))
    ang = pos[:, None] * theta[None, :]
    sign = np.tile(np.array([-1.0, 1.0]), RET_HEAD_DIM // 2)
    cos = np.repeat(np.cos(ang), 2, axis=1)
    sin = np.repeat(np.sin(ang), 2, axis=1) * sign
    hidx = np.arange(RET_HEADS, dtype=np.float64)
    lg_f = np.log1p(-np.exp2(-5.0 - hidx))[:, None]
    lg_b = np.log1p(-np.exp2(-5.5 - hidx))[:, None]
    idx = np.arange(RET_CHUNK, dtype=np.float64)
    diff = idx[:, None] - idx[None, :]
    dmat = np.where(diff >= 0, np.exp(lg_f[:, :, None] * np.maximum(diff, 0.0)),
                    np.exp(lg_b[:, :, None] * np.maximum(-diff, 0.0)))
    dec = np.stack([np.exp(lg_f * (idx + 1.0)), np.exp(lg_f * (RET_CHUNK - 1.0 - idx)),
                    np.exp(lg_b * (RET_CHUNK - idx)), np.exp(lg_b * idx)], axis=1)
    dec = np.broadcast_to(dec[..., None], (RET_HEADS, 4, RET_CHUNK, LANES))
    cdec = np.stack([np.exp(lg_f * RET_CHUNK), np.exp(lg_b * RET_CHUNK)], axis=1)
    cdec = np.broadcast_to(cdec, (RET_HEADS, 2, LANES))
    return tuple(jnp.asarray(np.ascontiguousarray(t), F32) for t in (cos, sin, dmat, dec, cdec))


def _retention(proj, gn, tables, layer):
    cos, sin, dmat, dec, cdec = tables
    blk = P_RET // RET_HEAD_DIM
    tok = lambda off: pl.BlockSpec((SEQ, RET_HEAD_DIM), lambda b, h: (b, blk + off * RET_HEADS + h))
    const2 = pl.BlockSpec((SEQ, RET_HEAD_DIM), lambda b, h: (0, 0))
    return pl.pallas_call(
        _ret_kernel,
        grid=(BATCH, RET_HEADS),
        in_specs=[
            tok(0), tok(1), tok(2), tok(3), const2, const2,
            pl.BlockSpec((1, RET_CHUNK, RET_CHUNK), lambda b, h: (h, 0, 0)),
            pl.BlockSpec((1, 4, RET_CHUNK, LANES), lambda b, h: (h, 0, 0, 0)),
            pl.BlockSpec((1, 2, LANES), lambda b, h: (h, 0, 0)),
            _layer_spec((1, RET_HEAD_DIM), lambda b, h: (layer, 0, h)),
        ],
        out_specs=pl.BlockSpec((SEQ, RET_HEAD_DIM), lambda b, h: (b, h)),
        out_shape=jax.ShapeDtypeStruct((TOKENS, RET_WIDTH), BF16),
        scratch_shapes=[pltpu.VMEM((SEQ, RET_HEAD_DIM), F32), pltpu.VMEM((SEQ, RET_HEAD_DIM), F32),
                        pltpu.VMEM((SEQ, RET_HEAD_DIM), F32)],
        compiler_params=_cparams(("parallel", "arbitrary")),
        name="retention",
    )(proj, proj, proj, proj, cos, sin, dmat, dec, cdec, gn)


LRU_SEGS = SUBLANES
LRU_SEG = SEQ // LRU_SEGS
LRU_PAD = SUBLANES
LRU_SLABS = LRU_WIDTH // LANES


def _softplus(x):
    return jnp.maximum(x, 0.0) + jnp.log1p(jnp.exp(-jnp.abs(x)))


def _lru_kernel(xc_ref, gc_ref, cw_ref, cb_ref, w_ref, bias_ref, lam_ref, o_ref,
                xpad_ref, a_ref, b_ref, hin_ref):
    zeros = jnp.zeros((LRU_PAD, LRU_WIDTH), F32)
    xpad_ref[0:LRU_PAD, :] = zeros
    xpad_ref[LRU_PAD + SEQ:, :] = zeros
    xpad_ref[LRU_PAD:LRU_PAD + SEQ, :] = xc_ref[...].astype(F32)

    cw = cw_ref[...]
    cb = cb_ref[...]
    half_k = (-0.5 * LRU_C * LOG2E) * _softplus(-lam_ref[...])
    row = lax.broadcasted_iota(jnp.int32, (LRU_SEG, LANES), 0)

    def gates(seg, first_dir):
        t0 = pl.multiple_of(seg * LRU_SEG, LRU_SEG)
        xw = xpad_ref[pl.ds(t0, LRU_SEG + 2 * LRU_PAD), :]
        taps = [xw[LRU_PAD - 2 + j:LRU_PAD - 2 + j + LRU_SEG, :] for j in range(4)]
        xf = taps[0] * cw[0:1] + taps[1] * cw[1:2] + taps[2] * cw[2:3] + taps[3] * cw[3:4] + cb
        for s in range(LRU_SLABS):
            cols = slice(s * LANES, (s + 1) * LANES)
            xs = xf[:, cols]
            xh = 0.5 * xs
            z = jnp.dot(xs.astype(BF16), w_ref[s], preferred_element_type=F32)
            for dr in range(2):
                zr = z[:, (2 * dr) * LANES:(2 * dr + 1) * LANES] + bias_ref[2 * dr:2 * dr + 1, cols]
                zi = z[:, (2 * dr + 1) * LANES:(2 * dr + 2) * LANES] + bias_ref[2 * dr + 1:2 * dr + 2, cols]
                hk = half_k[dr:dr + 1, cols]
                a = jnp.exp2(jnp.tanh(zr) * hk + hk)
                if first_dir == dr:
                    a = jnp.where(row == (0 if dr == 0 else LRU_SEG - 1), 0.0, a)
                om = 1.0 - a * a
                mult = om * lax.rsqrt(jnp.maximum(om, 1e-30))
                a_ref[dr, s, pl.ds(seg, LRU_SEG, stride=LRU_SEGS), :] = a
                b_ref[dr, s, pl.ds(seg, LRU_SEG, stride=LRU_SEGS), :] = mult * (jnp.tanh(zi) * xh + xh)

    gates(0, 0)
    lax.fori_loop(1, LRU_SEGS - 1, lambda seg, c: (gates(seg, None), c)[1], 0)
    gates(LRU_SEGS - 1, 1)

    def scan(j, carry):
        new = []
        for dr in range(2):
            r0 = pl.multiple_of((j if dr == 0 else LRU_SEG - 1 - j) * LRU_SEGS, LRU_SEGS)
            for s in range(LRU_SLABS):
                h, p = carry[dr * LRU_SLABS + s]
                a = a_ref[dr, s, pl.ds(r0, LRU_SEGS), :]
                h = a * h + b_ref[dr, s, pl.ds(r0, LRU_SEGS), :]
                p = a * p
                b_ref[dr, s, pl.ds(r0, LRU_SEGS), :] = h
                a_ref[dr, s, pl.ds(r0, LRU_SEGS), :] = p
                new.append((h, p))
        return tuple(new)

    init = (jnp.zeros((LRU_SEGS, LANES), F32), jnp.ones((LRU_SEGS, LANES), F32))
    ends = lax.fori_loop(0, LRU_SEG, scan, (init,) * (2 * LRU_SLABS), unroll=2)

    for dr in range(2):
        order = range(LRU_SEGS) if dr == 0 else range(LRU_SEGS - 1, -1, -1)
        for s in range(LRU_SLABS):
            h_end, p_end = ends[dr * LRU_SLABS + s]
            carry_in = jnp.zeros((1, LANES), F32)
            rows_in = [None] * LRU_SEGS
            for seg in order:
                rows_in[seg] = carry_in
                carry_in = h_end[seg:seg + 1, :] + p_end[seg:seg + 1, :] * carry_in
            hin_ref[dr, s] = jnp.concatenate(rows_in, axis=0)

    def finish(seg, carry):
        t0 = pl.multiple_of(seg * LRU_SEG, LRU_SEG)
        g = gc_ref[pl.ds(t0, LRU_SEG), :].astype(F32)
        g = jax.nn.gelu(g, approximate=True)
        for s in range(LRU_SLABS):
            cols = slice(s * LANES, (s + 1) * LANES)
            y = None
            for dr in range(2):
                h = b_ref[dr, s, pl.ds(seg, LRU_SEG, stride=LRU_SEGS), :]
                p = a_ref[dr, s, pl.ds(seg, LRU_SEG, stride=LRU_SEGS), :]
                h = h + p * hin_ref[dr, s, pl.ds(seg, 1), :]
                y = h if y is None else y + h
            o_ref[pl.ds(t0, LRU_SEG), cols] = (g[:, cols] * y).astype(o_ref.dtype)
        return carry

    lax.fori_loop(0, LRU_SEGS, finish, 0)


def _lru_weights(wa, wx):
    def slab(w):
        w = 0.5 * w.reshape(DEPTH, LRU_SLABS, 2, LRU_BLOCK_DIM, LRU_BLOCK_DIM)
        z = jnp.zeros_like(w[:, :, 0])
        top = jnp.concatenate([w[:, :, 0], z], axis=-1)
        bot = jnp.concatenate([z, w[:, :, 1]], axis=-1)
        return jnp.concatenate([top, bot], axis=-2)
    return jnp.concatenate([slab(wa[:, 0]), slab(wx[:, 0]), slab(wa[:, 1]), slab(wx[:, 1])],
                           axis=-1).astype(BF16)


def _rglru(proj, conv_w, conv_b, w4, bias4, lam, layer):
    blk = P_LRU // LRU_WIDTH
    full = lambda *shape: _layer_spec(shape, lambda b: (layer,) + (0,) * len(shape))
    state = pltpu.VMEM((2, LRU_SLABS, SEQ, LANES), F32)
    return pl.pallas_call(
        _lru_kernel,
        grid=(BATCH,),
        in_specs=[
            pl.BlockSpec((SEQ, LRU_WIDTH), lambda b: (b, blk)),
            pl.BlockSpec((SEQ, LRU_WIDTH), lambda b: (b, blk + 1)),
            full(4, LRU_WIDTH), full(1, LRU_WIDTH),
            full(LRU_SLABS, LANES, 4 * LANES), full(4, LRU_WIDTH), full(2, LRU_WIDTH),
        ],
        out_specs=pl.BlockSpec((SEQ, LRU_WIDTH), lambda b: (b, 0)),
        out_shape=jax.ShapeDtypeStruct((TOKENS, LRU_WIDTH), BF16),
        scratch_shapes=[pltpu.VMEM((SEQ + 2 * LRU_PAD, LRU_WIDTH), F32), state, state,
                        pltpu.VMEM((2, LRU_SLABS, LRU_SEGS, LANES), F32)],
        compiler_params=_cparams(("parallel",)),
        name="rglru",
    )(proj, proj, conv_w, conv_b, w4, bias4, lam)


MERGE_TM = 512
GATE_HALF = D_MODEL // 2


def _merge_kernel(x_ref, yn_ref, yr_ref, yl_ref, g00, g01, g10, g11, g20, g21, bg_ref, wb_ref, wo_ref, o_ref):
    merged = None
    branches = ((yn_ref, g00, g01), (yr_ref, g10, g11), (yl_ref, g20, g21))
    for n, (y_ref, ga_ref, gb_ref) in enumerate(branches):
        branch = jnp.dot(y_ref[...], wb_ref[n], preferred_element_type=F32)
        logits = jnp.concatenate([ga_ref[...], gb_ref[...]], axis=1).astype(F32)
        gate = jax.nn.sigmoid(logits + bg_ref[n:n + 1, :])
        merged = gate * branch if merged is None else merged + gate * branch
    o_ref[...] = x_ref[...] + jnp.dot(merged.astype(BF16), wo_ref[...], preferred_element_type=F32)


def _merge(x, y_na, y_ret, y_lru, proj, b_gate, w_branch, w_out, layer):
    ytile = pl.BlockSpec((MERGE_TM, NA_WIDTH), lambda i: (i, 0))
    gate = lambda c: pl.BlockSpec((MERGE_TM, GATE_HALF), lambda i: (i, P_GATE // GATE_HALF + c))
    xtile = pl.BlockSpec((MERGE_TM, D_MODEL), lambda i: (i, 0))
    return pl.pallas_call(
        _merge_kernel,
        grid=(TOKENS // MERGE_TM,),
        in_specs=[
            xtile, ytile, ytile, ytile, gate(0), gate(1), gate(2), gate(3), gate(4), gate(5),
            _layer_spec((N_BRANCH, D_MODEL), lambda i: (layer, 0, 0)),
            _layer_spec((N_BRANCH, NA_WIDTH, D_MODEL), lambda i: (layer, 0, 0, 0)),
            _layer_spec((D_MODEL, D_MODEL), lambda i: (layer, 0, 0)),
        ],
        out_specs=xtile,
        out_shape=jax.ShapeDtypeStruct((TOKENS, D_MODEL), F32),
        compiler_params=_cparams(("parallel",)),
        name="merge",
    )(x, y_na, y_ret, y_lru, proj, proj, proj, proj, proj, proj, b_gate, w_branch, w_out)


MLP_TM = 1024
MLP_TF = 1024


def _mlp_ple_kernel(x_ref, g_ref, wu_ref, wd_ref, p_ref, gp_ref, wg_ref, wp_ref, gf_ref, o_ref, h_ref,
                    *, final):
    k = pl.program_id(1)

    @pl.when(k == 0)
    def _():
        x = x_ref[...]
        h_ref[...] = (x * _rms_scale(x) * g_ref[...]).astype(BF16)
        o_ref[...] = x

    u = jnp.maximum(jnp.dot(h_ref[...], wu_ref[...], preferred_element_type=F32), 0.0)
    o_ref[...] += jnp.dot((u * u).astype(BF16), wd_ref[...], preferred_element_type=F32)

    @pl.when(k == pl.num_programs(1) - 1)
    def _():
        x = o_ref[...]
        h = (x * _rms_scale(x) * gp_ref[...]).astype(BF16)
        gate = jax.nn.sigmoid(jnp.dot(h, wg_ref[...], preferred_element_type=F32))
        emb = jnp.dot(p_ref[...].astype(BF16), wp_ref[...], preferred_element_type=F32)
        x = x + gate * emb
        if final:
            x = x * _rms_scale(x) * gf_ref[...]
        o_ref[...] = x


def _mlp_ple(x, g, w_up, w_down, p, g_ple, w_gate, w_ple, g_final, layer):
    xtile = pl.BlockSpec((MLP_TM, D_MODEL), lambda i, k: (i, 0))
    vec = _layer_spec((1, D_MODEL), lambda i, k: (layer, 0, 0))
    final = layer == DEPTH - 1
    return pl.pallas_call(
        functools.partial(_mlp_ple_kernel, final=final),
        grid=(TOKENS // MLP_TM, D_FF // MLP_TF),
        in_specs=[
            xtile,
            vec,
            _layer_spec((D_MODEL, MLP_TF), lambda i, k: (layer, 0, k)),
            _layer_spec((MLP_TF, D_MODEL), lambda i, k: (layer, k, 0)),
            _layer_spec((MLP_TM, PLE_DIM), lambda i, k: (layer, i, 0)),
            vec,
            _layer_spec((D_MODEL, D_MODEL), lambda i, k: (layer, 0, 0)),
            _layer_spec((PLE_DIM, D_MODEL), lambda i, k: (layer, 0, 0)),
            pl.BlockSpec((1, D_MODEL), lambda i, k: (0, 0)),
        ],
        out_specs=xtile,
        out_shape=jax.ShapeDtypeStruct((TOKENS, D_MODEL), F32),
        scratch_shapes=[pltpu.VMEM((MLP_TM, D_MODEL), BF16)],
        compiler_params=_cparams(("parallel", "arbitrary")),
        name="mlp_ple_final" if final else "mlp_ple",
    )(x, g, w_up, w_down, p, g_ple, w_gate, w_ple, g_final)


_IN_COL_SCALE = np.ones((IN_COLS,), np.float32)
_IN_COL_SCALE[P_NA:P_NA + NA_WIDTH] = NA_QSCALE


def kernel(x, p, g_mix, w_in, b_gate, na_rpb, ret_gn, conv_w, conv_b, lru_wa, lru_ba, lru_wx, lru_bx,
           lru_lambda, w_branch, w_out, g_mlp, w_up, w_down, g_ple, w_ple_gate, w_ple, g_final):
    rows = lambda v: v.reshape(DEPTH, 1, -1).astype(F32)
    w_in_b = (w_in * jnp.asarray(_IN_COL_SCALE)).astype(BF16)
    w_branch_b, w_out_b = w_branch.astype(BF16), w_out.astype(BF16)
    w_up_b, w_down_b = w_up.astype(BF16), w_down.astype(BF16)
    w_gate_b, w_ple_b = w_ple_gate.astype(BF16), w_ple.astype(BF16)
    g_mix_r, g_mlp_r, g_ple_r, gn_r, conv_b_r = rows(g_mix), rows(g_mlp), rows(g_ple), rows(ret_gn), rows(conv_b)
    rpb = _na_pad_rpb(na_rpb)
    lru_w4 = _lru_weights(lru_wa, lru_wx)
    lru_bias4 = 0.5 * jnp.stack([lru_ba[:, 0], lru_bx[:, 0], lru_ba[:, 1], lru_bx[:, 1]], axis=1).astype(F32)
    ret_tables = _ret_tables()
    pt = p.reshape(DEPTH, TOKENS, PLE_DIM)
    g_final_r = g_final.reshape(1, D_MODEL).astype(F32)

    xt = x.reshape(TOKENS, D_MODEL)
    for i in range(DEPTH):
        proj = _in_proj(xt, g_mix_r, w_in_b, i)
        y_na = _na(proj, rpb, i)
        y_ret = _retention(proj, gn_r, ret_tables, i)
        y_lru = _rglru(proj, conv_w.astype(F32), conv_b_r, lru_w4, lru_bias4, lru_lambda.astype(F32), i)
        xt = _merge(xt, y_na, y_ret, y_lru, proj, b_gate.astype(F32), w_branch_b, w_out_b, i)
        xt = _mlp_ple(xt, g_mlp_r, w_up_b, w_down_b, pt, g_ple_r, w_gate_b, w_ple_b, g_final_r, i)
    return xt.reshape(BATCH, SEQ, D_MODEL)
```

```python
import functools
import math

import numpy as np
import jax
import jax.numpy as jnp
from jax import lax
from jax.experimental import pallas as pl
from jax.experimental.pallas import tpu as pltpu

F32 = jnp.float32
BF16 = jnp.bfloat16

D_MODEL = 1024
BATCH = 8
SEQ = 2048
DEPTH = 2
TOKENS = BATCH * SEQ

GRID_W = 64
GRID_H = SEQ // GRID_W
WIN_H = 8
WIN_W = 16
NA_HEADS = 8
NA_HEAD_DIM = 64
NA_WIDTH = NA_HEADS * NA_HEAD_DIM
RET_HEADS = 4
RET_HEAD_DIM = 128
RET_WIDTH = RET_HEADS * RET_HEAD_DIM
RET_THETA_BASE = 10000.0
LRU_WIDTH = 512
LRU_BLOCKS = 8
LRU_BLOCK_DIM = LRU_WIDTH // LRU_BLOCKS
LRU_C = 8.0
N_BRANCH = 3
D_FF = 4 * D_MODEL
PLE_DIM = 256
RMS_EPS = 1e-6

P_NA = 0
P_RET = 3 * NA_WIDTH
P_LRU = P_RET + 4 * RET_WIDTH
P_GATE = P_LRU + 2 * LRU_WIDTH
IN_COLS = P_GATE + N_BRANCH * D_MODEL

LANES = 128
SUBLANES = 8
VMEM_LIMIT = 56 * 1024 * 1024

RET_CHUNK = 256
MASK_NEG = -1e30
LOG2E = math.log2(math.e)


def _cparams(sem):
    return pltpu.CompilerParams(dimension_semantics=sem, vmem_limit_bytes=VMEM_LIMIT)


def _rms_scale(x):
    return lax.rsqrt(jnp.mean(x * x, axis=-1, keepdims=True) + RMS_EPS)


def _layer_spec(shape, index_map):
    return pl.BlockSpec((None,) + shape, index_map)


IN_TM = 512
IN_TN = 512


def _inproj_kernel(x_ref, g_ref, w_ref, o_ref):
    x = x_ref[...]
    h = (x * _rms_scale(x) * g_ref[...]).astype(BF16)
    for c in range(IN_COLS // IN_TN):
        cols = slice(c * IN_TN, (c + 1) * IN_TN)
        o_ref[:, cols] = jnp.dot(h, w_ref[:, cols], preferred_element_type=F32).astype(o_ref.dtype)


def _in_proj(x, g, w, layer):
    return pl.pallas_call(
        _inproj_kernel,
        grid=(TOKENS // IN_TM,),
        in_specs=[
            pl.BlockSpec((IN_TM, D_MODEL), lambda i: (i, 0)),
            _layer_spec((1, D_MODEL), lambda i: (layer, 0, 0)),
            pl.BlockSpec((None, D_MODEL, IN_COLS), lambda i: (layer, 0, 0), pipeline_mode=pl.Buffered(1)),
        ],
        out_specs=pl.BlockSpec((IN_TM, IN_COLS), lambda i: (i, 0)),
        out_shape=jax.ShapeDtypeStruct((TOKENS, IN_COLS), BF16),
        compiler_params=_cparams(("parallel",)),
        name="in_proj",
    )(x, g, w)


NA_PAIRS = NA_HEADS // 2
NA_KEYS = WIN_H * GRID_W
NA_BIAS_ROWS = 2 * WIN_H - 2
NA_QSCALE = NA_HEAD_DIM ** -0.5 * LOG2E


def _na_kernel(q_ref, k_ref, v_ref, rpb_ref, o_ref, s_ref, t2_ref):
    lane = lax.broadcasted_iota(jnp.int32, (GRID_W, LANES), 1)
    lo = lane < NA_HEAD_DIM

    @pl.when(pl.program_id(0) == 0)
    def _():
        qcol = lax.broadcasted_iota(jnp.int32, (GRID_W, LANES), 0)
        kcol = jnp.where(lo, lane, lane - GRID_W)
        col_start = jnp.clip(qcol - WIN_W // 2, 0, GRID_W - WIN_W)
        off = kcol - col_start

        def build(ridx, carry):
            for h in range(NA_HEADS):
                xa = jnp.broadcast_to(rpb_ref[h, pl.ds(ridx, 1), :], (GRID_W, LANES))
                xb = jnp.broadcast_to(rpb_ref[h, pl.ds(ridx + 1, 1), :], (GRID_W, LANES))
                ta = pltpu.roll(xa, GRID_W + 1, 1, stride=1, stride_axis=0)
                tb = pltpu.roll(xb, 1, 1, stride=1, stride_axis=0)
                tile = jnp.where(off >= 0, jnp.where(off < WIN_W, jnp.where(lo, ta, tb), MASK_NEG), MASK_NEG)
                t2_ref[h // 2, ridx, (h % 2) * GRID_W:(h % 2 + 1) * GRID_W, :] = tile
            return carry

        lax.fori_loop(0, NA_BIAS_ROWS, build, 0)

    def window(r):
        if isinstance(r, int):
            r_start = min(max(r - WIN_H // 2, 0), GRID_H - WIN_H)
            return r_start * GRID_W, r_start - r + (WIN_H - 1)
        r_start = jnp.clip(r - WIN_H // 2, 0, GRID_H - WIN_H)
        return pl.multiple_of(r_start * GRID_W, GRID_W), r_start - r + (WIN_H - 1)

    def token_row(r):
        return r * GRID_W if isinstance(r, int) else pl.multiple_of(r * GRID_W, GRID_W)

    def scores(r, slot):
        kw0, d = window(r)
        q_r = q_ref[pl.ds(token_row(r), GRID_W), :]
        for p in range(NA_PAIRS):
            cols = slice(p * LANES, (p + 1) * LANES)
            qp = q_r[:, cols]
            zero = jnp.zeros_like(qp)
            q2 = jnp.concatenate([jnp.where(lo, qp, zero), jnp.where(lo, zero, qp)], axis=0)
            kp = k_ref[pl.ds(kw0, NA_KEYS), cols]
            s = lax.dot_general(q2, kp, (((1,), (1,)), ((), ())), preferred_element_type=F32)
            for c in range(NA_KEYS // LANES):
                kc = slice(c * LANES, (c + 1) * LANES)
                s_ref[slot, p, :, kc] = s[:, kc] + t2_ref[p, d + 2 * c]

    def attend(r, slot):
        kw0, _ = window(r)
        outs = []
        for p in range(NA_PAIRS):
            cols = slice(p * LANES, (p + 1) * LANES)
            s = s_ref[slot, p]
            e = jnp.exp2(s - jnp.max(s, axis=-1, keepdims=True))
            l = jnp.sum(e, axis=-1, keepdims=True)
            vp = v_ref[pl.ds(kw0, NA_KEYS), cols]
            o = jnp.dot(e.astype(BF16), vp, preferred_element_type=F32) / l
            outs.append(jnp.where(lo, o[:GRID_W], o[GRID_W:]))
        o_ref[pl.ds(token_row(r), GRID_W), :] = (
            jnp.concatenate(outs, axis=1).astype(o_ref.dtype))

    scores(0, 0)

    def two_rows(j, carry):
        r = 2 * j
        scores(r + 1, 1)
        attend(r, 0)
        scores(jnp.minimum(r + 2, GRID_H - 1), 0)
        attend(r + 1, 1)
        return carry

    lax.fori_loop(0, GRID_H // 2, two_rows, 0)


NA_RPB_ROWS = 2 * WIN_H
NA_RPB_LANE0 = GRID_W - WIN_W


def _na_pad_rpb(rpb):
    pad = ((0, 0), (0, 0), (0, NA_RPB_ROWS - rpb.shape[2]),
           (NA_RPB_LANE0, LANES - rpb.shape[3] - NA_RPB_LANE0))
    return jnp.pad(rpb.astype(F32) * LOG2E, pad)


def _na(proj, rpb, layer):
    blk = P_NA // NA_WIDTH
    return pl.pallas_call(
        _na_kernel,
        grid=(BATCH,),
        in_specs=[
            pl.BlockSpec((SEQ, NA_WIDTH), lambda b: (b, blk)),
            pl.BlockSpec((SEQ, NA_WIDTH), lambda b: (b, blk + 1)),
            pl.BlockSpec((SEQ, NA_WIDTH), lambda b: (b, blk + 2)),
            _layer_spec((NA_HEADS, NA_RPB_ROWS, LANES), lambda b: (layer, 0, 0, 0)),
        ],
        out_specs=pl.BlockSpec((SEQ, NA_WIDTH), lambda b: (b, 0)),
        out_shape=jax.ShapeDtypeStruct((TOKENS, NA_WIDTH), BF16),
        scratch_shapes=[pltpu.VMEM((2, NA_PAIRS, 2 * GRID_W, NA_KEYS), F32),
                        pltpu.VMEM((NA_PAIRS, NA_BIAS_ROWS, 2 * GRID_W, LANES), F32)],
        compiler_params=_cparams(("arbitrary",)),
        name="na",
    )(proj, proj, proj, rpb)


RET_NCHUNK = SEQ // RET_CHUNK


def _ret_kernel(q_ref, k_ref, v_ref, g_ref, cos_ref, sin_ref, dmat_ref, dec_ref, cdec_ref, gn_ref,
                o_ref, qr_ref, kr_ref, y_ref):
    lane = lax.broadcasted_iota(jnp.int32, (SEQ, LANES), 1)
    even = (lane & 1) == 0
    cos = cos_ref[...]
    sin = sin_ref[...]

    def rotate(t):
        partner = jnp.where(even, pltpu.roll(t, LANES - 1, 1), pltpu.roll(t, 1, 1))
        return t * cos + partner * sin

    qr_ref[...] = rotate(q_ref[...].astype(F32)) * (RET_HEAD_DIM ** -0.5)
    kr_ref[...] = rotate(k_ref[...].astype(F32))

    dmat = dmat_ref[0]
    q_f, k_f, q_b, k_b = dec_ref[0, 0], dec_ref[0, 1], dec_ref[0, 2], dec_ref[0, 3]
    c_f, c_b = cdec_ref[0, 0:1, :], cdec_ref[0, 1:2, :]

    def kv_update(state, cdec, kc, kdec, vc):
        kv = lax.dot_general((kc * kdec).astype(BF16), vc, (((0,), (0,)), ((), ())),
                             preferred_element_type=F32)
        return state * cdec + kv

    state = jnp.zeros((RET_HEAD_DIM, RET_HEAD_DIM), F32)
    for c in range(RET_NCHUNK):
        rows = slice(c * RET_CHUNK, (c + 1) * RET_CHUNK)
        qc, kc, vc = qr_ref[rows, :], kr_ref[rows, :], v_ref[rows, :]
        s = lax.dot_general(qc.astype(BF16), kc.astype(BF16), (((1,), (1,)), ((), ())),
                            preferred_element_type=F32) * dmat
        y = jnp.dot(s.astype(BF16), vc, preferred_element_type=F32)
        if c > 0:
            y = y + jnp.dot((qc * q_f).astype(BF16), state.astype(BF16), preferred_element_type=F32)
        if c < RET_NCHUNK - 1:
            state = kv_update(state, c_f, kc, k_f, vc)
        y_ref[rows, :] = y

    state = jnp.zeros((RET_HEAD_DIM, RET_HEAD_DIM), F32)
    for c in range(RET_NCHUNK - 1, -1, -1):
        rows = slice(c * RET_CHUNK, (c + 1) * RET_CHUNK)
        qc, kc, vc = qr_ref[rows, :], kr_ref[rows, :], v_ref[rows, :]
        if c < RET_NCHUNK - 1:
            y_ref[rows, :] += jnp.dot((qc * q_b).astype(BF16), state.astype(BF16),
                                      preferred_element_type=F32)
        if c > 0:
            state = kv_update(state, c_b, kc, k_b, vc)

    y = y_ref[...]
    y = y * _rms_scale(y) * gn_ref[...]
    g = g_ref[...].astype(F32)
    o_ref[...] = (g * jax.nn.sigmoid(g) * y).astype(o_ref.dtype)


def _ret_tables():
    pos = np.arange(SEQ, dtype=np.float64)
    theta = 1.0 / (RET_THETA_BASE ** np.linspace(0.0, 1.0, RET_HEAD_DIM // 2))
    ang = pos[:, None] * theta[None, :]
    sign = np.tile(np.array([-1.0, 1.0]), RET_HEAD_DIM // 2)
    cos = np.repeat(np.cos(ang), 2, axis=1)
    sin = np.repeat(np.sin(ang), 2, axis=1) * sign
    hidx = np.arange(RET_HEADS, dtype=np.float64)
    lg_f = np.log1p(-np.exp2(-5.0 - hidx))[:, None]
    lg_b = np.log1p(-np.exp2(-5.5 - hidx))[:, None]
    idx = np.arange(RET_CHUNK, dtype=np.float64)
    diff = idx[:, None] - idx[None, :]
    dmat = np.where(diff >= 0, np.exp(lg_f[:, :, None] * np.maximum(diff, 0.0)),
                    np.exp(lg_b[:, :, None] * np.maximum(-diff, 0.0)))
    dec = np.stack([np.exp(lg_f * (idx + 1.0)), np.exp(lg_f * (RET_CHUNK - 1.0 - idx)),
                    np.exp(lg_b * (RET_CHUNK - idx)), np.exp(lg_b * idx)], axis=1)
    dec = np.broadcast_to(dec[..., None], (RET_HEADS, 4, RET_CHUNK, LANES))
    cdec = np.stack([np.exp(lg_f * RET_CHUNK), np.exp(lg_b * RET_CHUNK)], axis=1)
    cdec = np.broadcast_to(cdec, (RET_HEADS, 2, LANES))
    return tuple(jnp.asarray(np.ascontiguousarray(t), F32) for t in (cos, sin, dmat, dec, cdec))


def _retention(proj, gn, tables, layer):
    cos, sin, dmat, dec, cdec = tables
    blk = P_RET // RET_HEAD_DIM
    tok = lambda off: pl.BlockSpec((SEQ, RET_HEAD_DIM), lambda b, h: (b, blk + off * RET_HEADS + h))
    const2 = pl.BlockSpec((SEQ, RET_HEAD_DIM), lambda b, h: (0, 0))
    return pl.pallas_call(
        _ret_kernel,
        grid=(BATCH, RET_HEADS),
        in_specs=[
            tok(0), tok(1), tok(2), tok(3), const2, const2,
            pl.BlockSpec((1, RET_CHUNK, RET_CHUNK), lambda b, h: (h, 0, 0)),
            pl.BlockSpec((1, 4, RET_CHUNK, LANES), lambda b, h: (h, 0, 0, 0)),
            pl.BlockSpec((1, 2, LANES), lambda b, h: (h, 0, 0)),
            _layer_spec((1, RET_HEAD_DIM), lambda b, h: (layer, 0, h)),
        ],
        out_specs=pl.BlockSpec((SEQ, RET_HEAD_DIM), lambda b, h: (b, h)),
        out_shape=jax.ShapeDtypeStruct((TOKENS, RET_WIDTH), BF16),
        scratch_shapes=[pltpu.VMEM((SEQ, RET_HEAD_DIM), F32), pltpu.VMEM((SEQ, RET_HEAD_DIM), F32),
                        pltpu.VMEM((SEQ, RET_HEAD_DIM), F32)],
        compiler_params=_cparams(("parallel", "arbitrary")),
        name="retention",
    )(proj, proj, proj, proj, cos, sin, dmat, dec, cdec, gn)


LRU_SEGS = SUBLANES
LRU_SEG = SEQ // LRU_SEGS
LRU_PAD = SUBLANES
LRU_SLABS = LRU_WIDTH // LANES


def _softplus(x):
    return jnp.maximum(x, 0.0) + jnp.log1p(jnp.exp(-jnp.abs(x)))


def _lru_kernel(xc_ref, gc_ref, cw_ref, cb_ref, w_ref, bias_ref, lam_ref, o_ref,
                xpad_ref, a_ref, b_ref, hin_ref):
    zeros = jnp.zeros((LRU_PAD, LRU_WIDTH), F32)
    xpad_ref[0:LRU_PAD, :] = zeros
    xpad_ref[LRU_PAD + SEQ:, :] = zeros
    xpad_ref[LRU_PAD:LRU_PAD + SEQ, :] = xc_ref[...].astype(F32)

    cw = cw_ref[...]
    cb = cb_ref[...]
    half_k = (-0.5 * LRU_C * LOG2E) * _softplus(-lam_ref[...])
    row = lax.broadcasted_iota(jnp.int32, (LRU_SEG, LANES), 0)

    def gates(seg, first_dir):
        t0 = pl.multiple_of(seg * LRU_SEG, LRU_SEG)
        xw = xpad_ref[pl.ds(t0, LRU_SEG + 2 * LRU_PAD), :]
        taps = [xw[LRU_PAD - 2 + j:LRU_PAD - 2 + j + LRU_SEG, :] for j in range(4)]
        xf = taps[0] * cw[0:1] + taps[1] * cw[1:2] + taps[2] * cw[2:3] + taps[3] * cw[3:4] + cb
        for s in range(LRU_SLABS):
            cols = slice(s * LANES, (s + 1) * LANES)
            xs = xf[:, cols]
            xh = 0.5 * xs
            z = jnp.dot(xs.astype(BF16), w_ref[s], preferred_element_type=F32)
            for dr in range(2):
                zr = z[:, (2 * dr) * LANES:(2 * dr + 1) * LANES] + bias_ref[2 * dr:2 * dr + 1, cols]
                zi = z[:, (2 * dr + 1) * LANES:(2 * dr + 2) * LANES] + bias_ref[2 * dr + 1:2 * dr + 2, cols]
                hk = half_k[dr:dr + 1, cols]
                a = jnp.exp2(jnp.tanh(zr) * hk + hk)
                if first_dir == dr:
                    a = jnp.where(row == (0 if dr == 0 else LRU_SEG - 1), 0.0, a)
                om = 1.0 - a * a
                mult = om * lax.rsqrt(jnp.maximum(om, 1e-30))
                a_ref[dr, s, pl.ds(seg, LRU_SEG, stride=LRU_SEGS), :] = a
                b_ref[dr, s, pl.ds(seg, LRU_SEG, stride=LRU_SEGS), :] = mult * (jnp.tanh(zi) * xh + xh)

    gates(0, 0)
    lax.fori_loop(1, LRU_SEGS - 1, lambda seg, c: (gates(seg, None), c)[1], 0)
    gates(LRU_SEGS - 1, 1)

    def scan(j, carry):
        new = []
        for dr in range(2):
            r0 = pl.multiple_of((j if dr == 0 else LRU_SEG - 1 - j) * LRU_SEGS, LRU_SEGS)
            for s in range(LRU_SLABS):
                h, p = carry[dr * LRU_SLABS + s]
                a = a_ref[dr, s, pl.ds(r0, LRU_SEGS), :]
                h = a * h + b_ref[dr, s, pl.ds(r0, LRU_SEGS), :]
                p = a * p
                b_ref[dr, s, pl.ds(r0, LRU_SEGS), :] = h
                a_ref[dr, s, pl.ds(r0, LRU_SEGS), :] = p
                new.append((h, p))
        return tuple(new)

    init = (jnp.zeros((LRU_SEGS, LANES), F32), jnp.ones((LRU_SEGS, LANES), F32))
    ends = lax.fori_loop(0, LRU_SEG, scan, (init,) * (2 * LRU_SLABS), unroll=2)

    for dr in range(2):
        order = range(LRU_SEGS) if dr == 0 else range(LRU_SEGS - 1, -1, -1)
        for s in range(LRU_SLABS):
            h_end, p_end = ends[dr * LRU_SLABS + s]
            carry_in = jnp.zeros((1, LANES), F32)
            rows_in = [None] * LRU_SEGS
            for seg in order:
                rows_in[seg] = carry_in
                carry_in = h_end[seg:seg + 1, :] + p_end[seg:seg + 1, :] * carry_in
            hin_ref[dr, s] = jnp.concatenate(rows_in, axis=0)

    def finish(seg, carry):
        t0 = pl.multiple_of(seg * LRU_SEG, LRU_SEG)
        g = gc_ref[pl.ds(t0, LRU_SEG), :].astype(F32)
        g = jax.nn.gelu(g, approximate=True)
        for s in range(LRU_SLABS):
            cols = slice(s * LANES, (s + 1) * LANES)
            y = None
            for dr in range(2):
                h = b_ref[dr, s, pl.ds(seg, LRU_SEG, stride=LRU_SEGS), :]
                p = a_ref[dr, s, pl.ds(seg, LRU_SEG, stride=LRU_SEGS), :]
                h = h + p * hin_ref[dr, s, pl.ds(seg, 1), :]
                y = h if y is None else y + h
            o_ref[pl.ds(t0, LRU_SEG), cols] = (g[:, cols] * y).astype(o_ref.dtype)
        return carry

    lax.fori_loop(0, LRU_SEGS, finish, 0)


def _lru_weights(wa, wx):
    def slab(w):
        w = 0.5 * w.reshape(DEPTH, LRU_SLABS, 2, LRU_BLOCK_DIM, LRU_BLOCK_DIM)
        z = jnp.zeros_like(w[:, :, 0])
        top = jnp.concatenate([w[:, :, 0], z], axis=-1)
        bot = jnp.concatenate([z, w[:, :, 1]], axis=-1)
        return jnp.concatenate([top, bot], axis=-2)
    return jnp.concatenate([slab(wa[:, 0]), slab(wx[:, 0]), slab(wa[:, 1]), slab(wx[:, 1])],
                           axis=-1).astype(BF16)


def _rglru(proj, conv_w, conv_b, w4, bias4, lam, layer):
    blk = P_LRU // LRU_WIDTH
    full = lambda *shape: _layer_spec(shape, lambda b: (layer,) + (0,) * len(shape))
    state = pltpu.VMEM((2, LRU_SLABS, SEQ, LANES), F32)
    return pl.pallas_call(
        _lru_kernel,
        grid=(BATCH,),
        in_specs=[
            pl.BlockSpec((SEQ, LRU_WIDTH), lambda b: (b, blk)),
            pl.BlockSpec((SEQ, LRU_WIDTH), lambda b: (b, blk + 1)),
            full(4, LRU_WIDTH), full(1, LRU_WIDTH),
            full(LRU_SLABS, LANES, 4 * LANES), full(4, LRU_WIDTH), full(2, LRU_WIDTH),
        ],
        out_specs=pl.BlockSpec((SEQ, LRU_WIDTH), lambda b: (b, 0)),
        out_shape=jax.ShapeDtypeStruct((TOKENS, LRU_WIDTH), BF16),
        scratch_shapes=[pltpu.VMEM((SEQ + 2 * LRU_PAD, LRU_WIDTH), F32), state, state,
                        pltpu.VMEM((2, LRU_SLABS, LRU_SEGS, LANES), F32)],
        compiler_params=_cparams(("parallel",)),
        name="rglru",
    )(proj, proj, conv_w, conv_b, w4, bias4, lam)


MERGE_TM = 512
GATE_HALF = D_MODEL // 2


def _merge_kernel(x_ref, yn_ref, yr_ref, yl_ref, g00, g01, g10, g11, g20, g21, bg_ref, wb_ref, wo_ref, o_ref):
    merged = None
    branches = ((yn_ref, g00, g01), (yr_ref, g10, g11), (yl_ref, g20, g21))
    for n, (y_ref, ga_ref, gb_ref) in enumerate(branches):
        branch = jnp.dot(y_ref[...], wb_ref[n], preferred_element_type=F32)
        logits = jnp.concatenate([ga_ref[...], gb_ref[...]], axis=1).astype(F32)
        gate = jax.nn.sigmoid(logits + bg_ref[n:n + 1, :])
        merged = gate * branch if merged is None else merged + gate * branch
    o_ref[...] = x_ref[...] + jnp.dot(merged.astype(BF16), wo_ref[...], preferred_element_type=F32)


def _merge(x, y_na, y_ret, y_lru, proj, b_gate, w_branch, w_out, layer):
    ytile = pl.BlockSpec((MERGE_TM, NA_WIDTH), lambda i: (i, 0))
    gate = lambda c: pl.BlockSpec((MERGE_TM, GATE_HALF), lambda i: (i, P_GATE // GATE_HALF + c))
    xtile = pl.BlockSpec((MERGE_TM, D_MODEL), lambda i: (i, 0))
    return pl.pallas_call(
        _merge_kernel,
        grid=(TOKENS // MERGE_TM,),
        in_specs=[
            xtile, ytile, ytile, ytile, gate(0), gate(1), gate(2), gate(3), gate(4), gate(5),
            _layer_spec((N_BRANCH, D_MODEL), lambda i: (layer, 0, 0)),
            _layer_spec((N_BRANCH, NA_WIDTH, D_MODEL), lambda i: (layer, 0, 0, 0)),
            _layer_spec((D_MODEL, D_MODEL), lambda i: (layer, 0, 0)),
        ],
        out_specs=xtile,
        out_shape=jax.ShapeDtypeStruct((TOKENS, D_MODEL), F32),
        compiler_params=_cparams(("parallel",)),
        name="merge",
    )(x, y_na, y_ret, y_lru, proj, proj, proj, proj, proj, proj, b_gate, w_branch, w_out)


MLP_TM = 1024
MLP_TF = 512


def _mlp_ple_kernel(x_ref, g_ref, wu_ref, wd_ref, p_ref, gp_ref, wg_ref, wp_ref, gf_ref, o_ref, *, final):
    x = x_ref[...]
    h = (x * _rms_scale(x) * g_ref[...]).astype(BF16)
    o_ref[...] = x
    for c in range(D_FF // MLP_TF):
        ff = slice(c * MLP_TF, (c + 1) * MLP_TF)
        u = jnp.maximum(jnp.dot(h, wu_ref[:, ff], preferred_element_type=F32), 0.0)
        o_ref[...] += jnp.dot((u * u).astype(BF16), wd_ref[ff, :], preferred_element_type=F32)
    x = o_ref[...]
    h = (x * _rms_scale(x) * gp_ref[...]).astype(BF16)
    gate = jax.nn.sigmoid(jnp.dot(h, wg_ref[...], preferred_element_type=F32))
    emb = jnp.dot(p_ref[...].astype(BF16), wp_ref[...], preferred_element_type=F32)
    x = x + gate * emb
    if final:
        x = x * _rms_scale(x) * gf_ref[...]
    o_ref[...] = x


def _mlp_ple(x, g, w_up, w_down, p, g_ple, w_gate, w_ple, g_final, layer):
    xtile = pl.BlockSpec((MLP_TM, D_MODEL), lambda i: (i, 0))
    vec = _layer_spec((1, D_MODEL), lambda i: (layer, 0, 0))
    resident = lambda *shape: pl.BlockSpec((None,) + shape, lambda i: (layer,) + (0,) * len(shape),
                                           pipeline_mode=pl.Buffered(1))
    final = layer == DEPTH - 1
    return pl.pallas_call(
        functools.partial(_mlp_ple_kernel, final=final),
        grid=(TOKENS // MLP_TM,),
        in_specs=[
            xtile,
            vec,
            resident(D_MODEL, D_FF),
            resident(D_FF, D_MODEL),
            _layer_spec((MLP_TM, PLE_DIM), lambda i: (layer, i, 0)),
            vec,
            resident(D_MODEL, D_MODEL),
            resident(PLE_DIM, D_MODEL),
            pl.BlockSpec((1, D_MODEL), lambda i: (0, 0)),
        ],
        out_specs=xtile,
        out_shape=jax.ShapeDtypeStruct((TOKENS, D_MODEL), F32),
        compiler_params=_cparams(("parallel",)),
        name="mlp_ple_final" if final else "mlp_ple",
    )(x, g, w_up, w_down, p, g_ple, w_gate, w_ple, g_final)


_IN_COL_SCALE = np.ones((IN_COLS,), np.float32)
_IN_COL_SCALE[P_NA:P_NA + NA_WIDTH] = NA_QSCALE


def kernel(x, p, g_mix, w_in, b_gate, na_rpb, ret_gn, conv_w, conv_b, lru_wa, lru_ba, lru_wx, lru_bx,
           lru_lambda, w_branch, w_out, g_mlp, w_up, w_down, g_ple, w_ple_gate, w_ple, g_final):
    rows = lambda v: v.reshape(DEPTH, 1, -1).astype(F32)
    w_in_b = (w_in * jnp.asarray(_IN_COL_SCALE)).astype(BF16)
    w_branch_b, w_out_b = w_branch.astype(BF16), w_out.astype(BF16)
    w_up_b, w_down_b = w_up.astype(BF16), w_down.astype(BF16)
    w_gate_b, w_ple_b = w_ple_gate.astype(BF16), w_ple.astype(BF16)
    g_mix_r, g_mlp_r, g_ple_r, gn_r, conv_b_r = rows(g_mix), rows(g_mlp), rows(g_ple), rows(ret_gn), rows(conv_b)
    rpb = _na_pad_rpb(na_rpb)
    lru_w4 = _lru_weights(lru_wa, lru_wx)
    lru_bias4 = 0.5 * jnp.stack([lru_ba[:, 0], lru_bx[:, 0], lru_ba[:, 1], lru_bx[:, 1]], axis=1).astype(F32)
    ret_tables = _ret_tables()
    pt = p.reshape(DEPTH, TOKENS, PLE_DIM)
    g_final_r = g_final.reshape(1, D_MODEL).astype(F32)

    xt = x.reshape(TOKENS, D_MODEL)
    for i in range(DEPTH):
        proj = _in_proj(xt, g_mix_r, w_in_b, i)
        y_na = _na(proj, rpb, i)
        y_ret = _retention(proj, gn_r, ret_tables, i)
        y_lru = _rglru(proj, conv_w.astype(F32), conv_b_r, lru_w4, lru_bias4, lru_lambda.astype(F32), i)
        xt = _merge(xt, y_na, y_ret, y_lru, proj, b_gate.astype(F32), w_branch_b, w_out_b, i)
        xt = _mlp_ple(xt, g_mlp_r, w_up_b, w_down_b, pt, g_ple_r, w_gate_b, w_ple_b, g_final_r, i)
    return xt.reshape(BATCH, SEQ, D_MODEL)
```

```python
import functools
import math

import numpy as np
import jax
import jax.numpy as jnp
from jax import lax
from jax.experimental import pallas as pl
from jax.experimental.pallas import tpu as pltpu

F32 = jnp.float32
BF16 = jnp.bfloat16

D_MODEL = 1024
BATCH = 8
SEQ = 2048
DEPTH = 2
TOKENS = BATCH * SEQ

GRID_W = 64
GRID_H = SEQ // GRID_W
WIN_H = 8
WIN_W = 16
NA_HEADS = 8
NA_HEAD_DIM = 64
NA_WIDTH = NA_HEADS * NA_HEAD_DIM
RET_HEADS = 4
RET_HEAD_DIM = 128
RET_WIDTH = RET_HEADS * RET_HEAD_DIM
RET_THETA_BASE = 10000.0
LRU_WIDTH = 512
LRU_BLOCKS = 8
LRU_BLOCK_DIM = LRU_WIDTH // LRU_BLOCKS
LRU_C = 8.0
N_BRANCH = 3
D_FF = 4 * D_MODEL
PLE_DIM = 256
RMS_EPS = 1e-6

P_NA = 0
P_RET = 3 * NA_WIDTH
P_LRU = P_RET + 4 * RET_WIDTH
P_GATE = P_LRU + 2 * LRU_WIDTH
IN_COLS = P_GATE + N_BRANCH * D_MODEL

LANES = 128
SUBLANES = 8
VMEM_LIMIT = 56 * 1024 * 1024

RET_CHUNK = 256
MASK_NEG = -1e30
LOG2E = math.log2(math.e)


def _cparams(sem):
    return pltpu.CompilerParams(dimension_semantics=sem, vmem_limit_bytes=VMEM_LIMIT)


def _rms_scale(x):
    return lax.rsqrt(jnp.mean(x * x, axis=-1, keepdims=True) + RMS_EPS)


def _layer_spec(shape, index_map):
    return pl.BlockSpec((None,) + shape, index_map)


IN_TM = 512
IN_TN = 512


def _inproj_kernel(x_ref, g_ref, w_ref, cos_ref, sin_ref, o_ref):
    x = x_ref[...]
    h = (x * _rms_scale(x) * g_ref[...]).astype(BF16)
    lane = lax.broadcasted_iota(jnp.int32, (IN_TM, LANES), 1)
    even = (lane & 1) == 0
    cos = cos_ref[...]
    sin = sin_ref[...]

    def rotate(t, scale):
        heads = []
        for hd in range(IN_TN // RET_HEAD_DIM):
            th = t[:, hd * RET_HEAD_DIM:(hd + 1) * RET_HEAD_DIM]
            partner = jnp.where(even, pltpu.roll(th, LANES - 1, 1), pltpu.roll(th, 1, 1))
            th = th * cos + partner * sin
            heads.append(th if scale is None else th * scale)
        return jnp.concatenate(heads, axis=1)

    ret_q, ret_k, ret_g = P_RET // IN_TN, P_RET // IN_TN + 1, P_RET // IN_TN + 3
    for c in range(IN_COLS // IN_TN):
        cols = slice(c * IN_TN, (c + 1) * IN_TN)
        r = jnp.dot(h, w_ref[:, cols], preferred_element_type=F32)
        if c == ret_q:
            r = rotate(r, RET_HEAD_DIM ** -0.5)
        elif c == ret_k:
            r = rotate(r, None)
        elif c == ret_g:
            r = r * jax.nn.sigmoid(r)
        o_ref[:, cols] = r.astype(o_ref.dtype)


def _in_proj(x, g, w, cos, sin, layer):
    rot = pl.BlockSpec((IN_TM, RET_HEAD_DIM), lambda i: (i % (SEQ // IN_TM), 0))
    return pl.pallas_call(
        _inproj_kernel,
        grid=(TOKENS // IN_TM,),
        in_specs=[
            pl.BlockSpec((IN_TM, D_MODEL), lambda i: (i, 0)),
            _layer_spec((1, D_MODEL), lambda i: (layer, 0, 0)),
            pl.BlockSpec((None, D_MODEL, IN_COLS), lambda i: (layer, 0, 0), pipeline_mode=pl.Buffered(1)),
            rot, rot,
        ],
        out_specs=pl.BlockSpec((IN_TM, IN_COLS), lambda i: (i, 0)),
        out_shape=jax.ShapeDtypeStruct((TOKENS, IN_COLS), BF16),
        compiler_params=_cparams(("parallel",)),
        name="in_proj",
    )(x, g, w, cos, sin)


NA_PAIRS = NA_HEADS // 2
NA_KEYS = WIN_H * GRID_W
NA_BIAS_ROWS = 2 * WIN_H - 2
NA_QSCALE = NA_HEAD_DIM ** -0.5 * LOG2E


def _na_kernel(q_ref, k_ref, v_ref, rpb_ref, o_ref, s_ref, t2_ref):
    lane = lax.broadcasted_iota(jnp.int32, (GRID_W, LANES), 1)
    lo = lane < NA_HEAD_DIM

    @pl.when(pl.program_id(0) == 0)
    def _():
        qcol = lax.broadcasted_iota(jnp.int32, (GRID_W, LANES), 0)
        kcol = jnp.where(lo, lane, lane - GRID_W)
        col_start = jnp.clip(qcol - WIN_W // 2, 0, GRID_W - WIN_W)
        off = kcol - col_start

        def build(ridx, carry):
            for h in range(NA_HEADS):
                xa = jnp.broadcast_to(rpb_ref[h, pl.ds(ridx, 1), :], (GRID_W, LANES))
                xb = jnp.broadcast_to(rpb_ref[h, pl.ds(ridx + 1, 1), :], (GRID_W, LANES))
                ta = pltpu.roll(xa, GRID_W + 1, 1, stride=1, stride_axis=0)
                tb = pltpu.roll(xb, 1, 1, stride=1, stride_axis=0)
                tile = jnp.where(off >= 0, jnp.where(off < WIN_W, jnp.where(lo, ta, tb), MASK_NEG), MASK_NEG)
                t2_ref[h // 2, ridx, (h % 2) * GRID_W:(h % 2 + 1) * GRID_W, :] = tile
            return carry

        lax.fori_loop(0, NA_BIAS_ROWS, build, 0)

    def window(r):
        if isinstance(r, int):
            r_start = min(max(r - WIN_H // 2, 0), GRID_H - WIN_H)
            return r_start * GRID_W, r_start - r + (WIN_H - 1)
        r_start = jnp.clip(r - WIN_H // 2, 0, GRID_H - WIN_H)
        return pl.multiple_of(r_start * GRID_W, GRID_W), r_start - r + (WIN_H - 1)

    def token_row(r):
        return r * GRID_W if isinstance(r, int) else pl.multiple_of(r * GRID_W, GRID_W)

    def scores(r, slot):
        kw0, d = window(r)
        q_r = q_ref[pl.ds(token_row(r), GRID_W), :]
        for p in range(NA_PAIRS):
            cols = slice(p * LANES, (p + 1) * LANES)
            qp = q_r[:, cols]
            zero = jnp.zeros_like(qp)
            q2 = jnp.concatenate([jnp.where(lo, qp, zero), jnp.where(lo, zero, qp)], axis=0)
            kp = k_ref[pl.ds(kw0, NA_KEYS), cols]
            s = lax.dot_general(q2, kp, (((1,), (1,)), ((), ())), preferred_element_type=F32)
            for c in range(NA_KEYS // LANES):
                kc = slice(c * LANES, (c + 1) * LANES)
                s_ref[slot, p, :, kc] = s[:, kc] + t2_ref[p, d + 2 * c]

    def attend(r, slot):
        kw0, _ = window(r)
        outs = []
        for p in range(NA_PAIRS):
            cols = slice(p * LANES, (p + 1) * LANES)
            s = s_ref[slot, p]
            e = jnp.exp2(s - jnp.max(s, axis=-1, keepdims=True))
            l = jnp.sum(e, axis=-1, keepdims=True)
            vp = v_ref[pl.ds(kw0, NA_KEYS), cols]
            o = jnp.dot(e.astype(BF16), vp, preferred_element_type=F32) / l
            outs.append(jnp.where(lo, o[:GRID_W], o[GRID_W:]))
        o_ref[pl.ds(token_row(r), GRID_W), :] = (
            jnp.concatenate(outs, axis=1).astype(o_ref.dtype))

    scores(0, 0)

    def two_rows(j, carry):
        r = 2 * j
        scores(r + 1, 1)
        attend(r, 0)
        scores(jnp.minimum(r + 2, GRID_H - 1), 0)
        attend(r + 1, 1)
        return carry

    lax.fori_loop(0, GRID_H // 2, two_rows, 0)


NA_RPB_ROWS = 2 * WIN_H
NA_RPB_LANE0 = GRID_W - WIN_W


def _na_pad_rpb(rpb):
    pad = ((0, 0), (0, 0), (0, NA_RPB_ROWS - rpb.shape[2]),
           (NA_RPB_LANE0, LANES - rpb.shape[3] - NA_RPB_LANE0))
    return jnp.pad(rpb.astype(F32) * LOG2E, pad)


def _na(proj, rpb, layer):
    blk = P_NA // NA_WIDTH
    return pl.pallas_call(
        _na_kernel,
        grid=(BATCH,),
        in_specs=[
            pl.BlockSpec((SEQ, NA_WIDTH), lambda b: (b, blk)),
            pl.BlockSpec((SEQ, NA_WIDTH), lambda b: (b, blk + 1)),
            pl.BlockSpec((SEQ, NA_WIDTH), lambda b: (b, blk + 2)),
            _layer_spec((NA_HEADS, NA_RPB_ROWS, LANES), lambda b: (layer, 0, 0, 0)),
        ],
        out_specs=pl.BlockSpec((SEQ, NA_WIDTH), lambda b: (b, 0)),
        out_shape=jax.ShapeDtypeStruct((TOKENS, NA_WIDTH), BF16),
        scratch_shapes=[pltpu.VMEM((2, NA_PAIRS, 2 * GRID_W, NA_KEYS), F32),
                        pltpu.VMEM((NA_PAIRS, NA_BIAS_ROWS, 2 * GRID_W, LANES), F32)],
        compiler_params=_cparams(("arbitrary",)),
        name="na",
    )(proj, proj, proj, rpb)


RET_NCHUNK = SEQ // RET_CHUNK


def _ret_kernel(q_ref, k_ref, v_ref, g_ref, dmat_ref, dec_ref, cdec_ref, gn_ref, o_ref, y_ref):
    dmat = dmat_ref[0]
    q_f, k_f, q_b, k_b = dec_ref[0, 0], dec_ref[0, 1], dec_ref[0, 2], dec_ref[0, 3]
    c_f, c_b = cdec_ref[0, 0:1, :], cdec_ref[0, 1:2, :]

    def decayed(t, dec):
        return (t.astype(F32) * dec).astype(BF16)

    def kv_update(state, cdec, kc, kdec, vc):
        kv = lax.dot_general(decayed(kc, kdec), vc, (((0,), (0,)), ((), ())), preferred_element_type=F32)
        return state * cdec + kv

    state = jnp.zeros((RET_HEAD_DIM, RET_HEAD_DIM), F32)
    for c in range(RET_NCHUNK):
        rows = slice(c * RET_CHUNK, (c + 1) * RET_CHUNK)
        qc, kc, vc = q_ref[rows, :], k_ref[rows, :], v_ref[rows, :]
        s = lax.dot_general(qc, kc, (((1,), (1,)), ((), ())), preferred_element_type=F32) * dmat
        y = jnp.dot(s.astype(BF16), vc, preferred_element_type=F32)
        if c > 0:
            y = y + jnp.dot(decayed(qc, q_f), state.astype(BF16), preferred_element_type=F32)
        if c < RET_NCHUNK - 1:
            state = kv_update(state, c_f, kc, k_f, vc)
        y_ref[rows, :] = y

    state = jnp.zeros((RET_HEAD_DIM, RET_HEAD_DIM), F32)
    for c in range(RET_NCHUNK - 1, -1, -1):
        rows = slice(c * RET_CHUNK, (c + 1) * RET_CHUNK)
        qc, kc, vc = q_ref[rows, :], k_ref[rows, :], v_ref[rows, :]
        if c < RET_NCHUNK - 1:
            y_ref[rows, :] += jnp.dot(decayed(qc, q_b), state.astype(BF16), preferred_element_type=F32)
        if c > 0:
            state = kv_update(state, c_b, kc, k_b, vc)

    y = y_ref[...]
    y = y * _rms_scale(y) * gn_ref[...]
    o_ref[...] = (g_ref[...].astype(F32) * y).astype(o_ref.dtype)


def _ret_tables():
    pos = np.arange(SEQ, dtype=np.float64)
    theta = 1.0 / (RET_THETA_BASE ** np.linspace(0.0, 1.0, RET_HEAD_DIM // 2))
    ang = pos[:, None] * theta[None, :]
    sign = np.tile(np.array([-1.0, 1.0]), RET_HEAD_DIM // 2)
    cos = np.repeat(np.cos(ang), 2, axis=1)
    sin = np.repeat(np.sin(ang), 2, axis=1) * sign
    hidx = np.arange(RET_HEADS, dtype=np.float64)
    lg_f = np.log1p(-np.exp2(-5.0 - hidx))[:, None]
    lg_b = np.log1p(-np.exp2(-5.5 - hidx))[:, None]
    idx = np.arange(RET_CHUNK, dtype=np.float64)
    diff = idx[:, None] - idx[None, :]
    dmat = np.where(diff >= 0, np.exp(lg_f[:, :, None] * np.maximum(diff, 0.0)),
                    np.exp(lg_b[:, :, None] * np.maximum(-diff, 0.0)))
    dec = np.stack([np.exp(lg_f * (idx + 1.0)), np.exp(lg_f * (RET_CHUNK - 1.0 - idx)),
                    np.exp(lg_b * (RET_CHUNK - idx)), np.exp(lg_b * idx)], axis=1)
    dec = np.broadcast_to(dec[..., None], (RET_HEADS, 4, RET_CHUNK, LANES))
    cdec = np.stack([np.exp(lg_f * RET_CHUNK), np.exp(lg_b * RET_CHUNK)], axis=1)
    cdec = np.broadcast_to(cdec, (RET_HEADS, 2, LANES))
    return tuple(jnp.asarray(np.ascontiguousarray(t), F32) for t in (cos, sin, dmat, dec, cdec))


def _retention(proj, gn, tables, layer):
    _, _, dmat, dec, cdec = tables
    blk = P_RET // RET_HEAD_DIM
    tok = lambda off: pl.BlockSpec((SEQ, RET_HEAD_DIM), lambda b, h: (b, blk + off * RET_HEADS + h))
    return pl.pallas_call(
        _ret_kernel,
        grid=(BATCH, RET_HEADS),
        in_specs=[
            tok(0), tok(1), tok(2), tok(3),
            pl.BlockSpec((1, RET_CHUNK, RET_CHUNK), lambda b, h: (h, 0, 0)),
            pl.BlockSpec((1, 4, RET_CHUNK, LANES), lambda b, h: (h, 0, 0, 0)),
            pl.BlockSpec((1, 2, LANES), lambda b, h: (h, 0, 0)),
            _layer_spec((1, RET_HEAD_DIM), lambda b, h: (layer, 0, h)),
        ],
        out_specs=pl.BlockSpec((SEQ, RET_HEAD_DIM), lambda b, h: (b, h)),
        out_shape=jax.ShapeDtypeStruct((TOKENS, RET_WIDTH), BF16),
        scratch_shapes=[pltpu.VMEM((SEQ, RET_HEAD_DIM), F32)],
        compiler_params=_cparams(("parallel", "arbitrary")),
        name="retention",
    )(proj, proj, proj, proj, dmat, dec, cdec, gn)


LRU_SEGS = SUBLANES
LRU_SEG = SEQ // LRU_SEGS
LRU_PAD = SUBLANES
LRU_SLABS = LRU_WIDTH // LANES


def _softplus(x):
    return jnp.maximum(x, 0.0) + jnp.log1p(jnp.exp(-jnp.abs(x)))


def _lru_kernel(xc_ref, gc_ref, cw_ref, cb_ref, w_ref, bias_ref, lam_ref, o_ref,
                xpad_ref, af_ref, ab_ref, bf_ref, bb_ref):
    a_ref, b_ref = (af_ref, ab_ref), (bf_ref, bb_ref)
    zeros = jnp.zeros((LRU_PAD, LRU_WIDTH), F32)
    xpad_ref[0:LRU_PAD, :] = zeros
    xpad_ref[LRU_PAD + SEQ:, :] = zeros
    xpad_ref[LRU_PAD:LRU_PAD + SEQ, :] = xc_ref[...].astype(F32)

    cw = cw_ref[...]
    cb = cb_ref[...]
    half_k = (-0.5 * LRU_C * LOG2E) * _softplus(-lam_ref[...])
    row = lax.broadcasted_iota(jnp.int32, (LRU_SEG, LANES), 0)

    def gates(seg, first_dir):
        t0 = pl.multiple_of(seg * LRU_SEG, LRU_SEG)
        xw = xpad_ref[pl.ds(t0, LRU_SEG + 2 * LRU_PAD), :]
        taps = [xw[LRU_PAD - 2 + j:LRU_PAD - 2 + j + LRU_SEG, :] for j in range(4)]
        xf = taps[0] * cw[0:1] + taps[1] * cw[1:2] + taps[2] * cw[2:3] + taps[3] * cw[3:4] + cb
        for s in range(LRU_SLABS):
            cols = slice(s * LANES, (s + 1) * LANES)
            xs = xf[:, cols]
            xh = 0.5 * xs
            z = jnp.dot(xs.astype(BF16), w_ref[s], preferred_element_type=F32)
            for dr in range(2):
                zr = z[:, (2 * dr) * LANES:(2 * dr + 1) * LANES] + bias_ref[2 * dr:2 * dr + 1, cols]
                zi = z[:, (2 * dr + 1) * LANES:(2 * dr + 2) * LANES] + bias_ref[2 * dr + 1:2 * dr + 2, cols]
                hk = half_k[dr:dr + 1, cols]
                a = jnp.exp2(jnp.tanh(zr) * hk + hk)
                if first_dir == dr:
                    a = jnp.where(row == (0 if dr == 0 else LRU_SEG - 1), 0.0, a)
                om = 1.0 - a * a
                mult = om * lax.rsqrt(jnp.maximum(om, 1e-30))
                a_ref[dr][s, pl.ds(seg, LRU_SEG, stride=LRU_SEGS), :] = a
                b_ref[dr][s, pl.ds(seg, LRU_SEG, stride=LRU_SEGS), :] = mult * (jnp.tanh(zi) * xh + xh)

    gates(0, 0)
    lax.fori_loop(1, LRU_SEGS - 1, lambda seg, c: (gates(seg, None), c)[1], 0)
    gates(LRU_SEGS - 1, 1)

    def group(dr, j):
        return pl.ds(pl.multiple_of((j if dr == 0 else LRU_SEG - 1 - j) * LRU_SEGS, LRU_SEGS), LRU_SEGS)

    def probe(j, carry):
        new = []
        for dr in range(2):
            for s in range(LRU_SLABS):
                h, p = carry[dr * LRU_SLABS + s]
                a = a_ref[dr][s, group(dr, j), :]
                new.append((a * h + b_ref[dr][s, group(dr, j), :], a * p))
        return tuple(new)

    init = (jnp.zeros((LRU_SEGS, LANES), F32), jnp.ones((LRU_SEGS, LANES), F32))
    ends = lax.fori_loop(0, LRU_SEG, probe, (init,) * (2 * LRU_SLABS), unroll=2)

    def carry_in(dr, s):
        h_end, p_end = ends[dr * LRU_SLABS + s]
        state = jnp.zeros((1, LANES), F32)
        rows_in = [None] * LRU_SEGS
        for seg in (range(LRU_SEGS) if dr == 0 else range(LRU_SEGS - 1, -1, -1)):
            rows_in[seg] = state
            state = h_end[seg:seg + 1, :] + p_end[seg:seg + 1, :] * state
        return jnp.concatenate(rows_in, axis=0)

    def scan(j, hs):
        new = []
        for dr in range(2):
            for s in range(LRU_SLABS):
                h = a_ref[dr][s, group(dr, j), :] * hs[dr * LRU_SLABS + s] + b_ref[dr][s, group(dr, j), :]
                b_ref[dr][s, group(dr, j), :] = h
                new.append(h)
        return tuple(new)

    lax.fori_loop(0, LRU_SEG, scan, tuple(carry_in(dr, s) for dr in range(2) for s in range(LRU_SLABS)),
                  unroll=2)

    def finish(seg, carry):
        t0 = pl.multiple_of(seg * LRU_SEG, LRU_SEG)
        g = jax.nn.gelu(gc_ref[pl.ds(t0, LRU_SEG), :].astype(F32), approximate=True)
        for s in range(LRU_SLABS):
            cols = slice(s * LANES, (s + 1) * LANES)
            y = (b_ref[0][s, pl.ds(seg, LRU_SEG, stride=LRU_SEGS), :]
                 + b_ref[1][s, pl.ds(seg, LRU_SEG, stride=LRU_SEGS), :])
            o_ref[pl.ds(t0, LRU_SEG), cols] = (g[:, cols] * y).astype(o_ref.dtype)
        return carry

    lax.fori_loop(0, LRU_SEGS, finish, 0)


def _lru_weights(wa, wx):
    def slab(w):
        w = 0.5 * w.reshape(DEPTH, LRU_SLABS, 2, LRU_BLOCK_DIM, LRU_BLOCK_DIM)
        z = jnp.zeros_like(w[:, :, 0])
        top = jnp.concatenate([w[:, :, 0], z], axis=-1)
        bot = jnp.concatenate([z, w[:, :, 1]], axis=-1)
        return jnp.concatenate([top, bot], axis=-2)
    return jnp.concatenate([slab(wa[:, 0]), slab(wx[:, 0]), slab(wa[:, 1]), slab(wx[:, 1])],
                           axis=-1).astype(BF16)


def _rglru(proj, conv_w, conv_b, w4, bias4, lam, layer):
    blk = P_LRU // LRU_WIDTH
    full = lambda *shape: _layer_spec(shape, lambda b: (layer,) + (0,) * len(shape))
    state = pltpu.VMEM((LRU_SLABS, SEQ, LANES), F32)
    return pl.pallas_call(
        _lru_kernel,
        grid=(BATCH,),
        in_specs=[
            pl.BlockSpec((SEQ, LRU_WIDTH), lambda b: (b, blk)),
            pl.BlockSpec((SEQ, LRU_WIDTH), lambda b: (b, blk + 1)),
            full(4, LRU_WIDTH), full(1, LRU_WIDTH),
            full(LRU_SLABS, LANES, 4 * LANES), full(4, LRU_WIDTH), full(2, LRU_WIDTH),
        ],
        out_specs=pl.BlockSpec((SEQ, LRU_WIDTH), lambda b: (b, 0)),
        out_shape=jax.ShapeDtypeStruct((TOKENS, LRU_WIDTH), BF16),
        scratch_shapes=[pltpu.VMEM((SEQ + 2 * LRU_PAD, LRU_WIDTH), F32), state, state, state, state],
        compiler_params=_cparams(("parallel",)),
        name="rglru",
    )(proj, proj, conv_w, conv_b, w4, bias4, lam)


MERGE_TM = 1024
GATE_HALF = D_MODEL // 2


def _merge_kernel(x_ref, yn_ref, yr_ref, yl_ref, g00, g01, g10, g11, g20, g21, bg_ref, wb_ref, wo_ref, o_ref):
    branches = ((yn_ref, (g00, g01)), (yr_ref, (g10, g11)), (yl_ref, (g20, g21)))
    halves = []
    for half in range(D_MODEL // GATE_HALF):
        cols = slice(half * GATE_HALF, (half + 1) * GATE_HALF)
        merged = None
        for n, (y_ref, gates) in enumerate(branches):
            branch = jnp.dot(y_ref[...], wb_ref[n, :, cols], preferred_element_type=F32)
            gate = jax.nn.sigmoid(gates[half][...].astype(F32) + bg_ref[n:n + 1, cols])
            merged = gate * branch if merged is None else merged + gate * branch
        halves.append(merged.astype(BF16))
    merged = jnp.concatenate(halves, axis=1)
    o_ref[...] = x_ref[...] + jnp.dot(merged, wo_ref[...], preferred_element_type=F32)


def _merge(x, y_na, y_ret, y_lru, proj, b_gate, w_branch, w_out, layer):
    ytile = pl.BlockSpec((MERGE_TM, NA_WIDTH), lambda i: (i, 0))
    gate = lambda c: pl.BlockSpec((MERGE_TM, GATE_HALF), lambda i: (i, P_GATE // GATE_HALF + c))
    xtile = pl.BlockSpec((MERGE_TM, D_MODEL), lambda i: (i, 0))
    resident = lambda *shape: pl.BlockSpec((None,) + shape, lambda i: (layer,) + (0,) * len(shape),
                                           pipeline_mode=pl.Buffered(1))
    return pl.pallas_call(
        _merge_kernel,
        grid=(TOKENS // MERGE_TM,),
        in_specs=[
            xtile, ytile, ytile, ytile, gate(0), gate(1), gate(2), gate(3), gate(4), gate(5),
            _layer_spec((N_BRANCH, D_MODEL), lambda i: (layer, 0, 0)),
            resident(N_BRANCH, NA_WIDTH, D_MODEL),
            resident(D_MODEL, D_MODEL),
        ],
        out_specs=xtile,
        out_shape=jax.ShapeDtypeStruct((TOKENS, D_MODEL), F32),
        compiler_params=_cparams(("parallel",)),
        name="merge",
    )(x, y_na, y_ret, y_lru, proj, proj, proj, proj, proj, proj, b_gate, w_branch, w_out)


MLP_TM = 1024
MLP_TF = 512


def _mlp_ple_kernel(x_ref, g_ref, wu_ref, wd_ref, p_ref, gp_ref, wg_ref, wp_ref, gf_ref, o_ref, *, final):
    x = x_ref[...]
    h = (x * _rms_scale(x) * g_ref[...]).astype(BF16)
    o_ref[...] = x
    for c in range(D_FF // MLP_TF):
        ff = slice(c * MLP_TF, (c + 1) * MLP_TF)
        u = jnp.maximum(jnp.dot(h, wu_ref[:, ff], preferred_element_type=F32), 0.0)
        o_ref[...] += jnp.dot((u * u).astype(BF16), wd_ref[ff, :], preferred_element_type=F32)
    x = o_ref[...]
    h = (x * _rms_scale(x) * gp_ref[...]).astype(BF16)
    gate = jax.nn.sigmoid(jnp.dot(h, wg_ref[...], preferred_element_type=F32))
    emb = jnp.dot(p_ref[...].astype(BF16), wp_ref[...], preferred_element_type=F32)
    x = x + gate * emb
    if final:
        x = x * _rms_scale(x) * gf_ref[...]
    o_ref[...] = x


def _mlp_ple(x, g, w_up, w_down, p, g_ple, w_gate, w_ple, g_final, layer):
    xtile = pl.BlockSpec((MLP_TM, D_MODEL), lambda i: (i, 0))
    vec = _layer_spec((1, D_MODEL), lambda i: (layer, 0, 0))
    resident = lambda *shape: pl.BlockSpec((None,) + shape, lambda i: (layer,) + (0,) * len(shape),
                                           pipeline_mode=pl.Buffered(1))
    final = layer == DEPTH - 1
    return pl.pallas_call(
        functools.partial(_mlp_ple_kernel, final=final),
        grid=(TOKENS // MLP_TM,),
        in_specs=[
            xtile,
            vec,
            resident(D_MODEL, D_FF),
            resident(D_FF, D_MODEL),
            _layer_spec((MLP_TM, PLE_DIM), lambda i: (layer, i, 0)),
            vec,
            resident(D_MODEL, D_MODEL),
            resident(PLE_DIM, D_MODEL),
            pl.BlockSpec((1, D_MODEL), lambda i: (0, 0)),
        ],
        out_specs=xtile,
        out_shape=jax.ShapeDtypeStruct((TOKENS, D_MODEL), F32),
        compiler_params=_cparams(("parallel",)),
        name="mlp_ple_final" if final else "mlp_ple",
    )(x, g, w_up, w_down, p, g_ple, w_gate, w_ple, g_final)


_IN_COL_SCALE = np.ones((IN_COLS,), np.float32)
_IN_COL_SCALE[P_NA:P_NA + NA_WIDTH] = NA_QSCALE


def kernel(x, p, g_mix, w_in, b_gate, na_rpb, ret_gn, conv_w, conv_b, lru_wa, lru_ba, lru_wx, lru_bx,
           lru_lambda, w_branch, w_out, g_mlp, w_up, w_down, g_ple, w_ple_gate, w_ple, g_final):
    rows = lambda v: v.reshape(DEPTH, 1, -1).astype(F32)
    w_in_b = (w_in * jnp.asarray(_IN_COL_SCALE)).astype(BF16)
    w_branch_b, w_out_b = w_branch.astype(BF16), w_out.astype(BF16)
    w_up_b, w_down_b = w_up.astype(BF16), w_down.astype(BF16)
    w_gate_b, w_ple_b = w_ple_gate.astype(BF16), w_ple.astype(BF16)
    g_mix_r, g_mlp_r, g_ple_r, gn_r, conv_b_r = rows(g_mix), rows(g_mlp), rows(g_ple), rows(ret_gn), rows(conv_b)
    rpb = _na_pad_rpb(na_rpb)
    lru_w4 = _lru_weights(lru_wa, lru_wx)
    lru_bias4 = 0.5 * jnp.stack([lru_ba[:, 0], lru_bx[:, 0], lru_ba[:, 1], lru_bx[:, 1]], axis=1).astype(F32)
    ret_tables = _ret_tables()
    pt = p.reshape(DEPTH, TOKENS, PLE_DIM)
    g_final_r = g_final.reshape(1, D_MODEL).astype(F32)

    xt = x.reshape(TOKENS, D_MODEL)
    for i in range(DEPTH):
        proj = _in_proj(xt, g_mix_r, w_in_b, ret_tables[0], ret_tables[1], i)
        y_na = _na(proj, rpb, i)
        y_ret = _retention(proj, gn_r, ret_tables, i)
        y_lru = _rglru(proj, conv_w.astype(F32), conv_b_r, lru_w4, lru_bias4, lru_lambda.astype(F32), i)
        xt = _merge(xt, y_na, y_ret, y_lru, proj, b_gate.astype(F32), w_branch_b, w_out_b, i)
        xt = _mlp_ple(xt, g_mlp_r, w_up_b, w_down_b, pt, g_ple_r, w_gate_b, w_ple_b, g_final_r, i)
    return xt.reshape(BATCH, SEQ, D_MODEL)
```

```python
import functools
import math

import numpy as np
import jax
import jax.numpy as jnp
from jax import lax
from jax.experimental import pallas as pl
from jax.experimental.pallas import tpu as pltpu

F32 = jnp.float32
BF16 = jnp.bfloat16

D_MODEL = 1024
BATCH = 8
SEQ = 2048
DEPTH = 2
TOKENS = BATCH * SEQ

GRID_W = 64
GRID_H = SEQ // GRID_W
WIN_H = 8
WIN_W = 16
NA_HEADS = 8
NA_HEAD_DIM = 64
NA_WIDTH = NA_HEADS * NA_HEAD_DIM
RET_HEADS = 4
RET_HEAD_DIM = 128
RET_WIDTH = RET_HEADS * RET_HEAD_DIM
RET_THETA_BASE = 10000.0
LRU_WIDTH = 512
LRU_BLOCKS = 8
LRU_BLOCK_DIM = LRU_WIDTH // LRU_BLOCKS
LRU_C = 8.0
N_BRANCH = 3
D_FF = 4 * D_MODEL
PLE_DIM = 256
RMS_EPS = 1e-6

P_NA = 0
P_RET = 3 * NA_WIDTH
P_LRU = P_RET + 4 * RET_WIDTH
P_GATE = P_LRU + 2 * LRU_WIDTH
IN_COLS = P_GATE + N_BRANCH * D_MODEL

LANES = 128
SUBLANES = 8
VMEM_LIMIT = 56 * 1024 * 1024

RET_CHUNK = 256
MASK_NEG = -1e30
LOG2E = math.log2(math.e)


def _cparams(sem):
    return pltpu.CompilerParams(dimension_semantics=sem, vmem_limit_bytes=VMEM_LIMIT)


def _rms_scale(x):
    return lax.rsqrt(jnp.mean(x * x, axis=-1, keepdims=True) + RMS_EPS)


def _layer_spec(shape, index_map):
    return pl.BlockSpec((None,) + shape, index_map)


IN_TM = 512
IN_TN = 512


def _inproj_kernel(x_ref, g_ref, w_ref, cos_ref, sin_ref, o_ref):
    x = x_ref[...]
    h = (x * _rms_scale(x) * g_ref[...]).astype(BF16)
    lane = lax.broadcasted_iota(jnp.int32, (IN_TM, LANES), 1)
    even = (lane & 1) == 0
    cos = cos_ref[...]
    sin = sin_ref[...]

    def rotate(t, scale):
        heads = []
        for hd in range(IN_TN // RET_HEAD_DIM):
            th = t[:, hd * RET_HEAD_DIM:(hd + 1) * RET_HEAD_DIM]
            partner = jnp.where(even, pltpu.roll(th, LANES - 1, 1), pltpu.roll(th, 1, 1))
            th = th * cos + partner * sin
            heads.append(th if scale is None else th * scale)
        return jnp.concatenate(heads, axis=1)

    ret_q, ret_k, ret_g = P_RET // IN_TN, P_RET // IN_TN + 1, P_RET // IN_TN + 3
    for c in range(IN_COLS // IN_TN):
        cols = slice(c * IN_TN, (c + 1) * IN_TN)
        r = jnp.dot(h, w_ref[:, cols], preferred_element_type=F32)
        if c == ret_q:
            r = rotate(r, RET_HEAD_DIM ** -0.5)
        elif c == ret_k:
            r = rotate(r, None)
        elif c == ret_g:
            r = r * jax.nn.sigmoid(r)
        o_ref[:, cols] = r.astype(o_ref.dtype)


def _in_proj(x, g, w, cos, sin, layer):
    rot = pl.BlockSpec((IN_TM, RET_HEAD_DIM), lambda i: (i % (SEQ // IN_TM), 0))
    return pl.pallas_call(
        _inproj_kernel,
        grid=(TOKENS // IN_TM,),
        in_specs=[
            pl.BlockSpec((IN_TM, D_MODEL), lambda i: (i, 0)),
            _layer_spec((1, D_MODEL), lambda i: (layer, 0, 0)),
            pl.BlockSpec((None, D_MODEL, IN_COLS), lambda i: (layer, 0, 0), pipeline_mode=pl.Buffered(1)),
            rot, rot,
        ],
        out_specs=pl.BlockSpec((IN_TM, IN_COLS), lambda i: (i, 0)),
        out_shape=jax.ShapeDtypeStruct((TOKENS, IN_COLS), BF16),
        compiler_params=_cparams(("parallel",)),
        name="in_proj",
    )(x, g, w, cos, sin)


NA_PAIRS = NA_HEADS // 2
NA_KEYS = WIN_H * GRID_W
NA_BIAS_ROWS = 2 * WIN_H - 2
NA_QSCALE = NA_HEAD_DIM ** -0.5 * LOG2E
NA_ROWS_PER_STEP = 8


def _na_kernel(q_ref, k_ref, v_ref, rpb_ref, o_ref, s_ref, t2_ref):
    lane = lax.broadcasted_iota(jnp.int32, (GRID_W, LANES), 1)
    lo = lane < NA_HEAD_DIM

    @pl.when(pl.program_id(0) == 0)
    def _():
        qcol = lax.broadcasted_iota(jnp.int32, (GRID_W, LANES), 0)
        kcol = jnp.where(lo, lane, lane - GRID_W)
        col_start = jnp.clip(qcol - WIN_W // 2, 0, GRID_W - WIN_W)
        off = kcol - col_start

        def build(ridx, carry):
            for h in range(NA_HEADS):
                xa = jnp.broadcast_to(rpb_ref[h, pl.ds(ridx, 1), :], (GRID_W, LANES))
                xb = jnp.broadcast_to(rpb_ref[h, pl.ds(ridx + 1, 1), :], (GRID_W, LANES))
                ta = pltpu.roll(xa, GRID_W + 1, 1, stride=1, stride_axis=0)
                tb = pltpu.roll(xb, 1, 1, stride=1, stride_axis=0)
                tile = jnp.where(off >= 0, jnp.where(off < WIN_W, jnp.where(lo, ta, tb), MASK_NEG), MASK_NEG)
                t2_ref[h // 2, ridx, (h % 2) * GRID_W:(h % 2 + 1) * GRID_W, :] = tile
            return carry

        lax.fori_loop(0, NA_BIAS_ROWS, build, 0)

    def window(r):
        if isinstance(r, int):
            r_start = min(max(r - WIN_H // 2, 0), GRID_H - WIN_H)
            return r_start * GRID_W, r_start - r + (WIN_H - 1)
        r_start = jnp.clip(r - WIN_H // 2, 0, GRID_H - WIN_H)
        return pl.multiple_of(r_start * GRID_W, GRID_W), r_start - r + (WIN_H - 1)

    def token_row(r):
        return r * GRID_W if isinstance(r, int) else pl.multiple_of(r * GRID_W, GRID_W)

    def scores(r, slot):
        kw0, d = window(r)
        q_r = q_ref[pl.ds(token_row(r), GRID_W), :]
        for p in range(NA_PAIRS):
            cols = slice(p * LANES, (p + 1) * LANES)
            qp = q_r[:, cols]
            zero = jnp.zeros_like(qp)
            q2 = jnp.concatenate([jnp.where(lo, qp, zero), jnp.where(lo, zero, qp)], axis=0)
            kp = k_ref[pl.ds(kw0, NA_KEYS), cols]
            s = lax.dot_general(q2, kp, (((1,), (1,)), ((), ())), preferred_element_type=F32)
            for c in range(NA_KEYS // LANES):
                kc = slice(c * LANES, (c + 1) * LANES)
                s_ref[slot, p, :, kc] = s[:, kc] + t2_ref[p, d + 2 * c]

    def attend(r, slot):
        kw0, _ = window(r)
        outs = []
        for p in range(NA_PAIRS):
            cols = slice(p * LANES, (p + 1) * LANES)
            s = s_ref[slot, p]
            e = jnp.exp2(s - jnp.max(s, axis=-1, keepdims=True))
            l = jnp.sum(e, axis=-1, keepdims=True)
            vp = v_ref[pl.ds(kw0, NA_KEYS), cols]
            o = jnp.dot(e.astype(BF16), vp, preferred_element_type=F32) / l
            outs.append(jnp.where(lo, o[:GRID_W], o[GRID_W:]))
        o_ref[pl.ds(token_row(r), GRID_W), :] = (
            jnp.concatenate(outs, axis=1).astype(o_ref.dtype))

    scores(0, 0)

    def rows(j, carry):
        r0 = NA_ROWS_PER_STEP * j
        for i in range(NA_ROWS_PER_STEP):
            scores(jnp.minimum(r0 + i + 1, GRID_H - 1), (i + 1) % 2)
            attend(r0 + i, i % 2)
        return carry

    lax.fori_loop(0, GRID_H // NA_ROWS_PER_STEP, rows, 0)


NA_RPB_ROWS = 2 * WIN_H
NA_RPB_LANE0 = GRID_W - WIN_W


def _na_pad_rpb(rpb):
    pad = ((0, 0), (0, 0), (0, NA_RPB_ROWS - rpb.shape[2]),
           (NA_RPB_LANE0, LANES - rpb.shape[3] - NA_RPB_LANE0))
    return jnp.pad(rpb.astype(F32) * LOG2E, pad)


def _na(proj, rpb, layer):
    blk = P_NA // NA_WIDTH
    return pl.pallas_call(
        _na_kernel,
        grid=(BATCH,),
        in_specs=[
            pl.BlockSpec((SEQ, NA_WIDTH), lambda b: (b, blk)),
            pl.BlockSpec((SEQ, NA_WIDTH), lambda b: (b, blk + 1)),
            pl.BlockSpec((SEQ, NA_WIDTH), lambda b: (b, blk + 2)),
            _layer_spec((NA_HEADS, NA_RPB_ROWS, LANES), lambda b: (layer, 0, 0, 0)),
        ],
        out_specs=pl.BlockSpec((SEQ, NA_WIDTH), lambda b: (b, 0)),
        out_shape=jax.ShapeDtypeStruct((TOKENS, NA_WIDTH), BF16),
        scratch_shapes=[pltpu.VMEM((2, NA_PAIRS, 2 * GRID_W, NA_KEYS), F32),
                        pltpu.VMEM((NA_PAIRS, NA_BIAS_ROWS, 2 * GRID_W, LANES), F32)],
        compiler_params=_cparams(("arbitrary",)),
        name="na",
    )(proj, proj, proj, rpb)


RET_NCHUNK = SEQ // RET_CHUNK


def _ret_kernel(q_ref, k_ref, v_ref, g_ref, dmat_ref, dec_ref, cdec_ref, gn_ref, o_ref, st_ref):
    def decayed(t, dec):
        return (t.astype(F32) * dec).astype(BF16)

    chunk = lambda c: slice(c * RET_CHUNK, (c + 1) * RET_CHUNK)
    last = RET_NCHUNK - 1

    for h in range(RET_HEADS):
        hd = slice(h * RET_HEAD_DIM, (h + 1) * RET_HEAD_DIM)
        dmat = dmat_ref[h]
        q_f, k_f, q_b, k_b = dec_ref[h, 0], dec_ref[h, 1], dec_ref[h, 2], dec_ref[h, 3]
        c_f, c_b = cdec_ref[h, 0:1, :], cdec_ref[h, 1:2, :]

        def summary(c, kdec):
            return lax.dot_general(decayed(k_ref[chunk(c), hd], kdec), v_ref[chunk(c), hd],
                                   (((0,), (0,)), ((), ())), preferred_element_type=F32)

        kv_f = [summary(c, k_f) for c in range(last)]
        kv_b = [None] + [summary(c, k_b) for c in range(1, RET_NCHUNK)]

        state = None
        for c in range(1, RET_NCHUNK):
            state = kv_f[c - 1] if state is None else state * c_f + kv_f[c - 1]
            st_ref[h, 0, c] = state.astype(BF16)
        state = None
        for c in range(last - 1, -1, -1):
            state = kv_b[c + 1] if state is None else state * c_b + kv_b[c + 1]
            st_ref[h, 1, c] = state.astype(BF16)

        gn = gn_ref[:, hd]
        for c in range(RET_NCHUNK):
            qc, kc, vc = q_ref[chunk(c), hd], k_ref[chunk(c), hd], v_ref[chunk(c), hd]
            s = lax.dot_general(qc, kc, (((1,), (1,)), ((), ())), preferred_element_type=F32) * dmat
            y = jnp.dot(s.astype(BF16), vc, preferred_element_type=F32)
            if c > 0:
                y = y + jnp.dot(decayed(qc, q_f), st_ref[h, 0, c], preferred_element_type=F32)
            if c < last:
                y = y + jnp.dot(decayed(qc, q_b), st_ref[h, 1, c], preferred_element_type=F32)
            y = y * _rms_scale(y) * gn
            o_ref[chunk(c), hd] = (g_ref[chunk(c), hd].astype(F32) * y).astype(o_ref.dtype)


def _ret_tables():
    pos = np.arange(SEQ, dtype=np.float64)
    theta = 1.0 / (RET_THETA_BASE ** np.linspace(0.0, 1.0, RET_HEAD_DIM // 2))
    ang = pos[:, None] * theta[None, :]
    sign = np.tile(np.array([-1.0, 1.0]), RET_HEAD_DIM // 2)
    cos = np.repeat(np.cos(ang), 2, axis=1)
    sin = np.repeat(np.sin(ang), 2, axis=1) * sign
    hidx = np.arange(RET_HEADS, dtype=np.float64)
    lg_f = np.log1p(-np.exp2(-5.0 - hidx))[:, None]
    lg_b = np.log1p(-np.exp2(-5.5 - hidx))[:, None]
    idx = np.arange(RET_CHUNK, dtype=np.float64)
    diff = idx[:, None] - idx[None, :]
    dmat = np.where(diff >= 0, np.exp(lg_f[:, :, None] * np.maximum(diff, 0.0)),
                    np.exp(lg_b[:, :, None] * np.maximum(-diff, 0.0)))
    dec = np.stack([np.exp(lg_f * (idx + 1.0)), np.exp(lg_f * (RET_CHUNK - 1.0 - idx)),
                    np.exp(lg_b * (RET_CHUNK - idx)), np.exp(lg_b * idx)], axis=1)
    dec = np.broadcast_to(dec[..., None], (RET_HEADS, 4, RET_CHUNK, LANES))
    cdec = np.stack([np.exp(lg_f * RET_CHUNK), np.exp(lg_b * RET_CHUNK)], axis=1)
    cdec = np.broadcast_to(cdec, (RET_HEADS, 2, LANES))
    return tuple(jnp.asarray(np.ascontiguousarray(t), F32) for t in (cos, sin, dmat, dec, cdec))


def _retention(proj, gn, tables, layer):
    _, _, dmat, dec, cdec = tables
    blk = P_RET // RET_WIDTH
    tok = lambda off: pl.BlockSpec((SEQ, RET_WIDTH), lambda b: (b, blk + off))
    return pl.pallas_call(
        _ret_kernel,
        grid=(BATCH,),
        in_specs=[
            tok(0), tok(1), tok(2), tok(3),
            pl.BlockSpec((RET_HEADS, RET_CHUNK, RET_CHUNK), lambda b: (0, 0, 0)),
            pl.BlockSpec((RET_HEADS, 4, RET_CHUNK, LANES), lambda b: (0, 0, 0, 0)),
            pl.BlockSpec((RET_HEADS, 2, LANES), lambda b: (0, 0, 0)),
            _layer_spec((1, RET_WIDTH), lambda b: (layer, 0, 0)),
        ],
        out_specs=pl.BlockSpec((SEQ, RET_WIDTH), lambda b: (b, 0)),
        out_shape=jax.ShapeDtypeStruct((TOKENS, RET_WIDTH), BF16),
        scratch_shapes=[pltpu.VMEM((RET_HEADS, 2, RET_NCHUNK, RET_HEAD_DIM, RET_HEAD_DIM), BF16)],
        compiler_params=_cparams(("parallel",)),
        name="retention",
    )(proj, proj, proj, proj, dmat, dec, cdec, gn)


LRU_SEGS = SUBLANES
LRU_SEG = SEQ // LRU_SEGS
LRU_PAD = SUBLANES
LRU_SLABS = LRU_WIDTH // LANES


def _softplus(x):
    return jnp.maximum(x, 0.0) + jnp.log1p(jnp.exp(-jnp.abs(x)))


def _lru_kernel(xc_ref, gc_ref, cw_ref, cb_ref, w_ref, bias_ref, lam_ref, o_ref,
                xpad_ref, af_ref, ab_ref, bf_ref, bb_ref):
    a_ref, b_ref = (af_ref, ab_ref), (bf_ref, bb_ref)
    zeros = jnp.zeros((LRU_PAD, LRU_WIDTH), F32)
    xpad_ref[0:LRU_PAD, :] = zeros
    xpad_ref[LRU_PAD + SEQ:, :] = zeros
    xpad_ref[LRU_PAD:LRU_PAD + SEQ, :] = xc_ref[...].astype(F32)

    cw = cw_ref[...]
    cb = cb_ref[...]
    half_k = (-0.5 * LRU_C * LOG2E) * _softplus(-lam_ref[...])
    row = lax.broadcasted_iota(jnp.int32, (LRU_SEG, LANES), 0)

    def gates(seg, first_dir):
        t0 = pl.multiple_of(seg * LRU_SEG, LRU_SEG)
        xw = xpad_ref[pl.ds(t0, LRU_SEG + 2 * LRU_PAD), :]
        taps = [xw[LRU_PAD - 2 + j:LRU_PAD - 2 + j + LRU_SEG, :] for j in range(4)]
        xf = taps[0] * cw[0:1] + taps[1] * cw[1:2] + taps[2] * cw[2:3] + taps[3] * cw[3:4] + cb
        for s in range(LRU_SLABS):
            cols = slice(s * LANES, (s + 1) * LANES)
            xs = xf[:, cols]
            xh = 0.5 * xs
            z = jnp.dot(xs.astype(BF16), w_ref[s], preferred_element_type=F32)
            for dr in range(2):
                zr = z[:, (2 * dr) * LANES:(2 * dr + 1) * LANES] + bias_ref[2 * dr:2 * dr + 1, cols]
                zi = z[:, (2 * dr + 1) * LANES:(2 * dr + 2) * LANES] + bias_ref[2 * dr + 1:2 * dr + 2, cols]
                hk = half_k[dr:dr + 1, cols]
                a = jnp.exp2(jnp.tanh(zr) * hk + hk)
                if first_dir == dr:
                    a = jnp.where(row == (0 if dr == 0 else LRU_SEG - 1), 0.0, a)
                om = 1.0 - a * a
                mult = om * lax.rsqrt(jnp.maximum(om, 1e-30))
                a_ref[dr][s, pl.ds(seg, LRU_SEG, stride=LRU_SEGS), :] = a
                b_ref[dr][s, pl.ds(seg, LRU_SEG, stride=LRU_SEGS), :] = mult * (jnp.tanh(zi) * xh + xh)

    gates(0, 0)
    lax.fori_loop(1, LRU_SEGS - 1, lambda seg, c: (gates(seg, None), c)[1], 0)
    gates(LRU_SEGS - 1, 1)

    def group(dr, j):
        return pl.ds(pl.multiple_of((j if dr == 0 else LRU_SEG - 1 - j) * LRU_SEGS, LRU_SEGS), LRU_SEGS)

    def probe(j, carry):
        new = []
        for dr in range(2):
            for s in range(LRU_SLABS):
                h, p = carry[dr * LRU_SLABS + s]
                a = a_ref[dr][s, group(dr, j), :]
                new.append((a * h + b_ref[dr][s, group(dr, j), :], a * p))
        return tuple(new)

    init = (jnp.zeros((LRU_SEGS, LANES), F32), jnp.ones((LRU_SEGS, LANES), F32))
    ends = lax.fori_loop(0, LRU_SEG, probe, (init,) * (2 * LRU_SLABS), unroll=2)

    def carry_in(dr, s):
        h_end, p_end = ends[dr * LRU_SLABS + s]
        state = jnp.zeros((1, LANES), F32)
        rows_in = [None] * LRU_SEGS
        for seg in (range(LRU_SEGS) if dr == 0 else range(LRU_SEGS - 1, -1, -1)):
            rows_in[seg] = state
            state = h_end[seg:seg + 1, :] + p_end[seg:seg + 1, :] * state
        return jnp.concatenate(rows_in, axis=0)

    def scan(j, hs):
        new = []
        for dr in range(2):
            for s in range(LRU_SLABS):
                h = a_ref[dr][s, group(dr, j), :] * hs[dr * LRU_SLABS + s] + b_ref[dr][s, group(dr, j), :]
                b_ref[dr][s, group(dr, j), :] = h
                new.append(h)
        return tuple(new)

    lax.fori_loop(0, LRU_SEG, scan, tuple(carry_in(dr, s) for dr in range(2) for s in range(LRU_SLABS)),
                  unroll=2)

    def finish(seg, carry):
        t0 = pl.multiple_of(seg * LRU_SEG, LRU_SEG)
        g = jax.nn.gelu(gc_ref[pl.ds(t0, LRU_SEG), :].astype(F32), approximate=True)
        for s in range(LRU_SLABS):
            cols = slice(s * LANES, (s + 1) * LANES)
            y = (b_ref[0][s, pl.ds(seg, LRU_SEG, stride=LRU_SEGS), :]
                 + b_ref[1][s, pl.ds(seg, LRU_SEG, stride=LRU_SEGS), :])
            o_ref[pl.ds(t0, LRU_SEG), cols] = (g[:, cols] * y).astype(o_ref.dtype)
        return carry

    lax.fori_loop(0, LRU_SEGS, finish, 0)


def _lru_weights(wa, wx):
    def slab(w):
        w = 0.5 * w.reshape(DEPTH, LRU_SLABS, 2, LRU_BLOCK_DIM, LRU_BLOCK_DIM)
        z = jnp.zeros_like(w[:, :, 0])
        top = jnp.concatenate([w[:, :, 0], z], axis=-1)
        bot = jnp.concatenate([z, w[:, :, 1]], axis=-1)
        return jnp.concatenate([top, bot], axis=-2)
    return jnp.concatenate([slab(wa[:, 0]), slab(wx[:, 0]), slab(wa[:, 1]), slab(wx[:, 1])],
                           axis=-1).astype(BF16)


def _rglru(proj, conv_w, conv_b, w4, bias4, lam, layer):
    blk = P_LRU // LRU_WIDTH
    full = lambda *shape: _layer_spec(shape, lambda b: (layer,) + (0,) * len(shape))
    state = pltpu.VMEM((LRU_SLABS, SEQ, LANES), F32)
    return pl.pallas_call(
        _lru_kernel,
        grid=(BATCH,),
        in_specs=[
            pl.BlockSpec((SEQ, LRU_WIDTH), lambda b: (b, blk)),
            pl.BlockSpec((SEQ, LRU_WIDTH), lambda b: (b, blk + 1)),
            full(4, LRU_WIDTH), full(1, LRU_WIDTH),
            full(LRU_SLABS, LANES, 4 * LANES), full(4, LRU_WIDTH), full(2, LRU_WIDTH),
        ],
        out_specs=pl.BlockSpec((SEQ, LRU_WIDTH), lambda b: (b, 0)),
        out_shape=jax.ShapeDtypeStruct((TOKENS, LRU_WIDTH), BF16),
        scratch_shapes=[pltpu.VMEM((SEQ + 2 * LRU_PAD, LRU_WIDTH), F32), state, state, state, state],
        compiler_params=_cparams(("parallel",)),
        name="rglru",
    )(proj, proj, conv_w, conv_b, w4, bias4, lam)


MERGE_TM = 1024
GATE_HALF = D_MODEL // 2


def _merge_kernel(x_ref, yn_ref, yr_ref, yl_ref, g00, g01, g10, g11, g20, g21, bg_ref, wb_ref, wo_ref, o_ref):
    branches = ((yn_ref, (g00, g01)), (yr_ref, (g10, g11)), (yl_ref, (g20, g21)))
    halves = []
    for half in range(D_MODEL // GATE_HALF):
        cols = slice(half * GATE_HALF, (half + 1) * GATE_HALF)
        merged = None
        for n, (y_ref, gates) in enumerate(branches):
            branch = jnp.dot(y_ref[...], wb_ref[n, :, cols], preferred_element_type=F32)
            gate = jax.nn.sigmoid(gates[half][...].astype(F32) + bg_ref[n:n + 1, cols])
            merged = gate * branch if merged is None else merged + gate * branch
        halves.append(merged.astype(BF16))
    merged = jnp.concatenate(halves, axis=1)
    o_ref[...] = x_ref[...] + jnp.dot(merged, wo_ref[...], preferred_element_type=F32)


def _merge(x, y_na, y_ret, y_lru, proj, b_gate, w_branch, w_out, layer):
    ytile = pl.BlockSpec((MERGE_TM, NA_WIDTH), lambda i: (i, 0))
    gate = lambda c: pl.BlockSpec((MERGE_TM, GATE_HALF), lambda i: (i, P_GATE // GATE_HALF + c))
    xtile = pl.BlockSpec((MERGE_TM, D_MODEL), lambda i: (i, 0))
    resident = lambda *shape: pl.BlockSpec((None,) + shape, lambda i: (layer,) + (0,) * len(shape),
                                           pipeline_mode=pl.Buffered(1))
    return pl.pallas_call(
        _merge_kernel,
        grid=(TOKENS // MERGE_TM,),
        in_specs=[
            xtile, ytile, ytile, ytile, gate(0), gate(1), gate(2), gate(3), gate(4), gate(5),
            _layer_spec((N_BRANCH, D_MODEL), lambda i: (layer, 0, 0)),
            resident(N_BRANCH, NA_WIDTH, D_MODEL),
            resident(D_MODEL, D_MODEL),
        ],
        out_specs=xtile,
        out_shape=jax.ShapeDtypeStruct((TOKENS, D_MODEL), F32),
        compiler_params=_cparams(("parallel",)),
        name="merge",
    )(x, y_na, y_ret, y_lru, proj, proj, proj, proj, proj, proj, b_gate, w_branch, w_out)


MLP_TM = 1024
MLP_TF = 512


def _mlp_ple_kernel(x_ref, g_ref, wu_ref, wd_ref, p_ref, gp_ref, wg_ref, wp_ref, gf_ref, o_ref, *, final):
    x = x_ref[...]
    h = (x * _rms_scale(x) * g_ref[...]).astype(BF16)
    o_ref[...] = x
    for c in range(D_FF // MLP_TF):
        ff = slice(c * MLP_TF, (c + 1) * MLP_TF)
        u = jnp.maximum(jnp.dot(h, wu_ref[:, ff], preferred_element_type=F32), 0.0)
        o_ref[...] += jnp.dot((u * u).astype(BF16), wd_ref[ff, :], preferred_element_type=F32)
    x = o_ref[...]
    h = (x * _rms_scale(x) * gp_ref[...]).astype(BF16)
    gate = jax.nn.sigmoid(jnp.dot(h, wg_ref[...], preferred_element_type=F32))
    emb = jnp.dot(p_ref[...].astype(BF16), wp_ref[...], preferred_element_type=F32)
    x = x + gate * emb
    if final:
        x = x * _rms_scale(x) * gf_ref[...]
    o_ref[...] = x


def _mlp_ple(x, g, w_up, w_down, p, g_ple, w_gate, w_ple, g_final, layer):
    xtile = pl.BlockSpec((MLP_TM, D_MODEL), lambda i: (i, 0))
    vec = _layer_spec((1, D_MODEL), lambda i: (layer, 0, 0))
    resident = lambda *shape: pl.BlockSpec((None,) + shape, lambda i: (layer,) + (0,) * len(shape),
                                           pipeline_mode=pl.Buffered(1))
    final = layer == DEPTH - 1
    return pl.pallas_call(
        functools.partial(_mlp_ple_kernel, final=final),
        grid=(TOKENS // MLP_TM,),
        in_specs=[
            xtile,
            vec,
            resident(D_MODEL, D_FF),
            resident(D_FF, D_MODEL),
            _layer_spec((MLP_TM, PLE_DIM), lambda i: (layer, i, 0)),
            vec,
            resident(D_MODEL, D_MODEL),
            resident(PLE_DIM, D_MODEL),
            pl.BlockSpec((1, D_MODEL), lambda i: (0, 0)),
        ],
        out_specs=xtile,
        out_shape=jax.ShapeDtypeStruct((TOKENS, D_MODEL), F32),
        compiler_params=_cparams(("parallel",)),
        name="mlp_ple_final" if final else "mlp_ple",
    )(x, g, w_up, w_down, p, g_ple, w_gate, w_ple, g_final)


_IN_COL_SCALE = np.ones((IN_COLS,), np.float32)
_IN_COL_SCALE[P_NA:P_NA + NA_WIDTH] = NA_QSCALE


def kernel(x, p, g_mix, w_in, b_gate, na_rpb, ret_gn, conv_w, conv_b, lru_wa, lru_ba, lru_wx, lru_bx,
           lru_lambda, w_branch, w_out, g_mlp, w_up, w_down, g_ple, w_ple_gate, w_ple, g_final):
    rows = lambda v: v.reshape(DEPTH, 1, -1).astype(F32)
    w_in_b = (w_in * jnp.asarray(_IN_COL_SCALE)).astype(BF16)
    w_branch_b, w_out_b = w_branch.astype(BF16), w_out.astype(BF16)
    w_up_b, w_down_b = w_up.astype(BF16), w_down.astype(BF16)
    w_gate_b, w_ple_b = w_ple_gate.astype(BF16), w_ple.astype(BF16)
    g_mix_r, g_mlp_r, g_ple_r, gn_r, conv_b_r = rows(g_mix), rows(g_mlp), rows(g_ple), rows(ret_gn), rows(conv_b)
    rpb = _na_pad_rpb(na_rpb)
    lru_w4 = _lru_weights(lru_wa, lru_wx)
    lru_bias4 = 0.5 * jnp.stack([lru_ba[:, 0], lru_bx[:, 0], lru_ba[:, 1], lru_bx[:, 1]], axis=1).astype(F32)
    ret_tables = _ret_tables()
    pt = p.reshape(DEPTH, TOKENS, PLE_DIM)
    g_final_r = g_final.reshape(1, D_MODEL).astype(F32)

    xt = x.reshape(TOKENS, D_MODEL)
    for i in range(DEPTH):
        proj = _in_proj(xt, g_mix_r, w_in_b, ret_tables[0], ret_tables[1], i)
        y_na = _na(proj, rpb, i)
        y_ret = _retention(proj, gn_r, ret_tables, i)
        y_lru = _rglru(proj, conv_w.astype(F32), conv_b_r, lru_w4, lru_bias4, lru_lambda.astype(F32), i)
        xt = _merge(xt, y_na, y_ret, y_lru, proj, b_gate.astype(F32), w_branch_b, w_out_b, i)
        xt = _mlp_ple(xt, g_mlp_r, w_up_b, w_down_b, pt, g_ple_r, w_gate_b, w_ple_b, g_final_r, i)
    return xt.reshape(BATCH, SEQ, D_MODEL)
```

```python
import functools
import math

import numpy as np
import jax
import jax.numpy as jnp
from jax import lax
from jax.experimental import pallas as pl
from jax.experimental.pallas import tpu as pltpu

F32 = jnp.float32
BF16 = jnp.bfloat16

D_MODEL = 1024
BATCH = 8
SEQ = 2048
DEPTH = 2
TOKENS = BATCH * SEQ

GRID_W = 64
GRID_H = SEQ // GRID_W
WIN_H = 8
WIN_W = 16
NA_HEADS = 8
NA_HEAD_DIM = 64
NA_WIDTH = NA_HEADS * NA_HEAD_DIM
RET_HEADS = 4
RET_HEAD_DIM = 128
RET_WIDTH = RET_HEADS * RET_HEAD_DIM
RET_THETA_BASE = 10000.0
LRU_WIDTH = 512
LRU_BLOCKS = 8
LRU_BLOCK_DIM = LRU_WIDTH // LRU_BLOCKS
LRU_C = 8.0
N_BRANCH = 3
D_FF = 4 * D_MODEL
PLE_DIM = 256
RMS_EPS = 1e-6

P_NA = 0
P_RET = 3 * NA_WIDTH
P_LRU = P_RET + 4 * RET_WIDTH
P_GATE = P_LRU + 2 * LRU_WIDTH
IN_COLS = P_GATE + N_BRANCH * D_MODEL

LANES = 128
SUBLANES = 8
VMEM_LIMIT = 56 * 1024 * 1024

RET_CHUNK = 256
MASK_NEG = -1e30
LOG2E = math.log2(math.e)


def _cparams(sem):
    return pltpu.CompilerParams(dimension_semantics=sem, vmem_limit_bytes=VMEM_LIMIT)


def _rms_scale(x):
    return lax.rsqrt(jnp.mean(x * x, axis=-1, keepdims=True) + RMS_EPS)


def _layer_spec(shape, index_map):
    return pl.BlockSpec((None,) + shape, index_map)


IN_TM = 512
IN_TN = 512


def _inproj_kernel(x_ref, g_ref, w_ref, cos_ref, sin_ref, o_ref):
    x = x_ref[...]
    h = (x * _rms_scale(x) * g_ref[...]).astype(BF16)
    lane = lax.broadcasted_iota(jnp.int32, (IN_TM, LANES), 1)
    even = (lane & 1) == 0
    cos = cos_ref[...]
    sin = sin_ref[...]

    def rotate(t, scale):
        heads = []
        for hd in range(IN_TN // RET_HEAD_DIM):
            th = t[:, hd * RET_HEAD_DIM:(hd + 1) * RET_HEAD_DIM]
            partner = jnp.where(even, pltpu.roll(th, LANES - 1, 1), pltpu.roll(th, 1, 1))
            th = th * cos + partner * sin
            heads.append(th if scale is None else th * scale)
        return jnp.concatenate(heads, axis=1)

    ret_q, ret_k, ret_g = P_RET // IN_TN, P_RET // IN_TN + 1, P_RET // IN_TN + 3
    for c in range(IN_COLS // IN_TN):
        cols = slice(c * IN_TN, (c + 1) * IN_TN)
        r = jnp.dot(h, w_ref[:, cols], preferred_element_type=F32)
        if c == ret_q:
            r = rotate(r, RET_HEAD_DIM ** -0.5)
        elif c == ret_k:
            r = rotate(r, None)
        elif c == ret_g:
            r = r * jax.nn.sigmoid(r)
        o_ref[:, cols] = r.astype(o_ref.dtype)


def _in_proj(x, g, w, cos, sin, layer):
    rot = pl.BlockSpec((IN_TM, RET_HEAD_DIM), lambda i: (i % (SEQ // IN_TM), 0))
    return pl.pallas_call(
        _inproj_kernel,
        grid=(TOKENS // IN_TM,),
        in_specs=[
            pl.BlockSpec((IN_TM, D_MODEL), lambda i: (i, 0)),
            _layer_spec((1, D_MODEL), lambda i: (layer, 0, 0)),
            pl.BlockSpec((None, D_MODEL, IN_COLS), lambda i: (layer, 0, 0), pipeline_mode=pl.Buffered(1)),
            rot, rot,
        ],
        out_specs=pl.BlockSpec((IN_TM, IN_COLS), lambda i: (i, 0)),
        out_shape=jax.ShapeDtypeStruct((TOKENS, IN_COLS), BF16),
        compiler_params=_cparams(("parallel",)),
        name="in_proj",
    )(x, g, w, cos, sin)


NA_PAIRS = NA_HEADS // 2
NA_KEYS = WIN_H * GRID_W
NA_BIAS_ROWS = 2 * WIN_H - 2
NA_QSCALE = NA_HEAD_DIM ** -0.5 * LOG2E
NA_ROWS_PER_STEP = 8


def _na_kernel(q_ref, k_ref, v_ref, rpb_ref, o_ref, s_ref, t2_ref):
    lane = lax.broadcasted_iota(jnp.int32, (GRID_W, LANES), 1)
    lo = lane < NA_HEAD_DIM

    @pl.when(pl.program_id(0) == 0)
    def _():
        qcol = lax.broadcasted_iota(jnp.int32, (GRID_W, LANES), 0)
        kcol = jnp.where(lo, lane, lane - GRID_W)
        col_start = jnp.clip(qcol - WIN_W // 2, 0, GRID_W - WIN_W)
        off = kcol - col_start

        def build(ridx, carry):
            for h in range(NA_HEADS):
                xa = jnp.broadcast_to(rpb_ref[h, pl.ds(ridx, 1), :], (GRID_W, LANES))
                xb = jnp.broadcast_to(rpb_ref[h, pl.ds(ridx + 1, 1), :], (GRID_W, LANES))
                ta = pltpu.roll(xa, GRID_W + 1, 1, stride=1, stride_axis=0)
                tb = pltpu.roll(xb, 1, 1, stride=1, stride_axis=0)
                tile = jnp.where(off >= 0, jnp.where(off < WIN_W, jnp.where(lo, ta, tb), MASK_NEG), MASK_NEG)
                t2_ref[h // 2, ridx, (h % 2) * GRID_W:(h % 2 + 1) * GRID_W, :] = tile
            return carry

        lax.fori_loop(0, NA_BIAS_ROWS, build, 0)

    def window(r):
        if isinstance(r, int):
            r_start = min(max(r - WIN_H // 2, 0), GRID_H - WIN_H)
            return r_start * GRID_W, r_start - r + (WIN_H - 1)
        r_start = jnp.clip(r - WIN_H // 2, 0, GRID_H - WIN_H)
        return pl.multiple_of(r_start * GRID_W, GRID_W), r_start - r + (WIN_H - 1)

    def token_row(r):
        return r * GRID_W if isinstance(r, int) else pl.multiple_of(r * GRID_W, GRID_W)

    def scores(r, slot):
        kw0, d = window(r)
        q_r = q_ref[pl.ds(token_row(r), GRID_W), :]
        for p in range(NA_PAIRS):
            cols = slice(p * LANES, (p + 1) * LANES)
            qp = q_r[:, cols]
            zero = jnp.zeros_like(qp)
            q2 = jnp.concatenate([jnp.where(lo, qp, zero), jnp.where(lo, zero, qp)], axis=0)
            kp = k_ref[pl.ds(kw0, NA_KEYS), cols]
            s = lax.dot_general(q2, kp, (((1,), (1,)), ((), ())), preferred_element_type=F32)
            for c in range(NA_KEYS // LANES):
                kc = slice(c * LANES, (c + 1) * LANES)
                s_ref[slot, p, :, kc] = s[:, kc] + t2_ref[p, d + 2 * c]

    def attend(r, slot):
        kw0, _ = window(r)
        outs = []
        for p in range(NA_PAIRS):
            cols = slice(p * LANES, (p + 1) * LANES)
            s = s_ref[slot, p]
            e = jnp.exp2(s - jnp.max(s, axis=-1, keepdims=True))
            l = jnp.sum(e, axis=-1, keepdims=True)
            vp = v_ref[pl.ds(kw0, NA_KEYS), cols]
            o = jnp.dot(e.astype(BF16), vp, preferred_element_type=F32) / l
            outs.append(jnp.where(lo, o[:GRID_W], o[GRID_W:]))
        o_ref[pl.ds(token_row(r), GRID_W), :] = (
            jnp.concatenate(outs, axis=1).astype(o_ref.dtype))

    scores(0, 0)

    def rows(j, carry):
        r0 = NA_ROWS_PER_STEP * j
        for i in range(NA_ROWS_PER_STEP):
            scores(jnp.minimum(r0 + i + 1, GRID_H - 1), (i + 1) % 2)
            attend(r0 + i, i % 2)
        return carry

    lax.fori_loop(0, GRID_H // NA_ROWS_PER_STEP, rows, 0)


NA_RPB_ROWS = 2 * WIN_H
NA_RPB_LANE0 = GRID_W - WIN_W


def _na_pad_rpb(rpb):
    pad = ((0, 0), (0, 0), (0, NA_RPB_ROWS - rpb.shape[2]),
           (NA_RPB_LANE0, LANES - rpb.shape[3] - NA_RPB_LANE0))
    return jnp.pad(rpb.astype(F32) * LOG2E, pad)


def _na(proj, rpb, layer):
    blk = P_NA // NA_WIDTH
    return pl.pallas_call(
        _na_kernel,
        grid=(BATCH,),
        in_specs=[
            pl.BlockSpec((SEQ, NA_WIDTH), lambda b: (b, blk)),
            pl.BlockSpec((SEQ, NA_WIDTH), lambda b: (b, blk + 1)),
            pl.BlockSpec((SEQ, NA_WIDTH), lambda b: (b, blk + 2)),
            _layer_spec((NA_HEADS, NA_RPB_ROWS, LANES), lambda b: (layer, 0, 0, 0)),
        ],
        out_specs=pl.BlockSpec((SEQ, NA_WIDTH), lambda b: (b, 0)),
        out_shape=jax.ShapeDtypeStruct((TOKENS, NA_WIDTH), BF16),
        scratch_shapes=[pltpu.VMEM((2, NA_PAIRS, 2 * GRID_W, NA_KEYS), F32),
                        pltpu.VMEM((NA_PAIRS, NA_BIAS_ROWS, 2 * GRID_W, LANES), F32)],
        compiler_params=_cparams(("arbitrary",)),
        name="na",
    )(proj, proj, proj, rpb)


RET_NCHUNK = SEQ // RET_CHUNK


def _ret_kernel(q_ref, k_ref, v_ref, g_ref, dmat_ref, dec_ref, cdec_ref, gn_ref, o_ref, st_ref):
    def decayed(t, dec):
        return (t.astype(F32) * dec).astype(BF16)

    chunk = lambda c: slice(c * RET_CHUNK, (c + 1) * RET_CHUNK)
    last = RET_NCHUNK - 1

    for h in range(RET_HEADS):
        hd = slice(h * RET_HEAD_DIM, (h + 1) * RET_HEAD_DIM)
        dmat = dmat_ref[h]
        q_f, k_f, q_b, k_b = dec_ref[h, 0], dec_ref[h, 1], dec_ref[h, 2], dec_ref[h, 3]
        c_f, c_b = cdec_ref[h, 0:1, :], cdec_ref[h, 1:2, :]

        def summary(c, kdec):
            return lax.dot_general(decayed(k_ref[chunk(c), hd], kdec), v_ref[chunk(c), hd],
                                   (((0,), (0,)), ((), ())), preferred_element_type=F32)

        kv_f = [summary(c, k_f) for c in range(last)]
        kv_b = [None] + [summary(c, k_b) for c in range(1, RET_NCHUNK)]

        state = None
        for c in range(1, RET_NCHUNK):
            state = kv_f[c - 1] if state is None else state * c_f + kv_f[c - 1]
            st_ref[h, 0, c] = state.astype(BF16)
        state = None
        for c in range(last - 1, -1, -1):
            state = kv_b[c + 1] if state is None else state * c_b + kv_b[c + 1]
            st_ref[h, 1, c] = state.astype(BF16)

        gn = gn_ref[:, hd]
        for c in range(RET_NCHUNK):
            qc, kc, vc = q_ref[chunk(c), hd], k_ref[chunk(c), hd], v_ref[chunk(c), hd]
            s = lax.dot_general(qc, kc, (((1,), (1,)), ((), ())), preferred_element_type=F32) * dmat
            y = jnp.dot(s.astype(BF16), vc, preferred_element_type=F32)
            if c > 0:
                y = y + jnp.dot(decayed(qc, q_f), st_ref[h, 0, c], preferred_element_type=F32)
            if c < last:
                y = y + jnp.dot(decayed(qc, q_b), st_ref[h, 1, c], preferred_element_type=F32)
            y = y * _rms_scale(y) * gn
            o_ref[chunk(c), hd] = (g_ref[chunk(c), hd].astype(F32) * y).astype(o_ref.dtype)


def _ret_tables():
    pos = np.arange(SEQ, dtype=np.float64)
    theta = 1.0 / (RET_THETA_BASE ** np.linspace(0.0, 1.0, RET_HEAD_DIM // 2))
    ang = pos[:, None] * theta[None, :]
    sign = np.tile(np.array([-1.0, 1.0]), RET_HEAD_DIM // 2)
    cos = np.repeat(np.cos(ang), 2, axis=1)
    sin = np.repeat(np.sin(ang), 2, axis=1) * sign
    hidx = np.arange(RET_HEADS, dtype=np.float64)
    lg_f = np.log1p(-np.exp2(-5.0 - hidx))[:, None]
    lg_b = np.log1p(-np.exp2(-5.5 - hidx))[:, None]
    idx = np.arange(RET_CHUNK, dtype=np.float64)
    diff = idx[:, None] - idx[None, :]
    dmat = np.where(diff >= 0, np.exp(lg_f[:, :, None] * np.maximum(diff, 0.0)),
                    np.exp(lg_b[:, :, None] * np.maximum(-diff, 0.0)))
    dec = np.stack([np.exp(lg_f * (idx + 1.0)), np.exp(lg_f * (RET_CHUNK - 1.0 - idx)),
                    np.exp(lg_b * (RET_CHUNK - idx)), np.exp(lg_b * idx)], axis=1)
    dec = np.broadcast_to(dec[..., None], (RET_HEADS, 4, RET_CHUNK, LANES))
    cdec = np.stack([np.exp(lg_f * RET_CHUNK), np.exp(lg_b * RET_CHUNK)], axis=1)
    cdec = np.broadcast_to(cdec, (RET_HEADS, 2, LANES))
    return tuple(jnp.asarray(np.ascontiguousarray(t), F32) for t in (cos, sin, dmat, dec, cdec))


def _retention(proj, gn, tables, layer):
    _, _, dmat, dec, cdec = tables
    blk = P_RET // RET_WIDTH
    tok = lambda off: pl.BlockSpec((SEQ, RET_WIDTH), lambda b: (b, blk + off))
    return pl.pallas_call(
        _ret_kernel,
        grid=(BATCH,),
        in_specs=[
            tok(0), tok(1), tok(2), tok(3),
            pl.BlockSpec((RET_HEADS, RET_CHUNK, RET_CHUNK), lambda b: (0, 0, 0)),
            pl.BlockSpec((RET_HEADS, 4, RET_CHUNK, LANES), lambda b: (0, 0, 0, 0)),
            pl.BlockSpec((RET_HEADS, 2, LANES), lambda b: (0, 0, 0)),
            _layer_spec((1, RET_WIDTH), lambda b: (layer, 0, 0)),
        ],
        out_specs=pl.BlockSpec((SEQ, RET_WIDTH), lambda b: (b, 0)),
        out_shape=jax.ShapeDtypeStruct((TOKENS, RET_WIDTH), BF16),
        scratch_shapes=[pltpu.VMEM((RET_HEADS, 2, RET_NCHUNK, RET_HEAD_DIM, RET_HEAD_DIM), BF16)],
        compiler_params=_cparams(("parallel",)),
        name="retention",
    )(proj, proj, proj, proj, dmat, dec, cdec, gn)


LRU_SEGS = SUBLANES
LRU_SEG = SEQ // LRU_SEGS
LRU_PAD = SUBLANES
LRU_SLABS = LRU_WIDTH // LANES


def _softplus(x):
    return jnp.maximum(x, 0.0) + jnp.log1p(jnp.exp(-jnp.abs(x)))


def _lru_kernel(xc_ref, gc_ref, cw_ref, cb_ref, w_ref, bias_ref, lam_ref, o_ref,
                xpad_ref, af_ref, ab_ref, bf_ref, bb_ref):
    a_ref, b_ref = (af_ref, ab_ref), (bf_ref, bb_ref)
    zeros = jnp.zeros((LRU_PAD, LRU_WIDTH), F32)
    xpad_ref[0:LRU_PAD, :] = zeros
    xpad_ref[LRU_PAD + SEQ:, :] = zeros
    xpad_ref[LRU_PAD:LRU_PAD + SEQ, :] = xc_ref[...].astype(F32)

    cw = cw_ref[...]
    cb = cb_ref[...]
    half_k = (-0.5 * LRU_C * LOG2E) * _softplus(-lam_ref[...])
    row = lax.broadcasted_iota(jnp.int32, (LRU_SEG, LANES), 0)

    def gates(seg, first_dir):
        t0 = pl.multiple_of(seg * LRU_SEG, LRU_SEG)
        xw = xpad_ref[pl.ds(t0, LRU_SEG + 2 * LRU_PAD), :]
        taps = [xw[LRU_PAD - 2 + j:LRU_PAD - 2 + j + LRU_SEG, :] for j in range(4)]
        xf = taps[0] * cw[0:1] + taps[1] * cw[1:2] + taps[2] * cw[2:3] + taps[3] * cw[3:4] + cb
        for s in range(LRU_SLABS):
            cols = slice(s * LANES, (s + 1) * LANES)
            xs = xf[:, cols]
            xh = 0.5 * xs
            z = jnp.dot(xs.astype(BF16), w_ref[s], preferred_element_type=F32)
            for dr in range(2):
                zr = z[:, (2 * dr) * LANES:(2 * dr + 1) * LANES] + bias_ref[2 * dr:2 * dr + 1, cols]
                zi = z[:, (2 * dr + 1) * LANES:(2 * dr + 2) * LANES] + bias_ref[2 * dr + 1:2 * dr + 2, cols]
                hk = half_k[dr:dr + 1, cols]
                a = jnp.exp2(jnp.tanh(zr) * hk + hk)
                if first_dir == dr:
                    a = jnp.where(row == (0 if dr == 0 else LRU_SEG - 1), 0.0, a)
                om = 1.0 - a * a
                mult = om * lax.rsqrt(jnp.maximum(om, 1e-30))
                a_ref[dr][s, pl.ds(seg, LRU_SEG, stride=LRU_SEGS), :] = a
                b_ref[dr][s, pl.ds(seg, LRU_SEG, stride=LRU_SEGS), :] = mult * (jnp.tanh(zi) * xh + xh)

    gates(0, 0)
    lax.fori_loop(1, LRU_SEGS - 1, lambda seg, c: (gates(seg, None), c)[1], 0)
    gates(LRU_SEGS - 1, 1)

    def group(dr, j):
        return pl.ds(pl.multiple_of((j if dr == 0 else LRU_SEG - 1 - j) * LRU_SEGS, LRU_SEGS), LRU_SEGS)

    def probe(j, carry):
        new = []
        for dr in range(2):
            for s in range(LRU_SLABS):
                h, p = carry[dr * LRU_SLABS + s]
                a = a_ref[dr][s, group(dr, j), :]
                new.append((a * h + b_ref[dr][s, group(dr, j), :], a * p))
        return tuple(new)

    init = (jnp.zeros((LRU_SEGS, LANES), F32), jnp.ones((LRU_SEGS, LANES), F32))
    ends = lax.fori_loop(0, LRU_SEG, probe, (init,) * (2 * LRU_SLABS), unroll=2)

    def carry_in(dr, s):
        h_end, p_end = ends[dr * LRU_SLABS + s]
        state = jnp.zeros((1, LANES), F32)
        rows_in = [None] * LRU_SEGS
        for seg in (range(LRU_SEGS) if dr == 0 else range(LRU_SEGS - 1, -1, -1)):
            rows_in[seg] = state
            state = h_end[seg:seg + 1, :] + p_end[seg:seg + 1, :] * state
        return jnp.concatenate(rows_in, axis=0)

    def scan(j, hs):
        new = []
        for dr in range(2):
            for s in range(LRU_SLABS):
                h = a_ref[dr][s, group(dr, j), :] * hs[dr * LRU_SLABS + s] + b_ref[dr][s, group(dr, j), :]
                b_ref[dr][s, group(dr, j), :] = h
                new.append(h)
        return tuple(new)

    lax.fori_loop(0, LRU_SEG, scan, tuple(carry_in(dr, s) for dr in range(2) for s in range(LRU_SLABS)),
                  unroll=2)

    def finish(seg, carry):
        t0 = pl.multiple_of(seg * LRU_SEG, LRU_SEG)
        g = jax.nn.gelu(gc_ref[pl.ds(t0, LRU_SEG), :].astype(F32), approximate=True)
        for s in range(LRU_SLABS):
            cols = slice(s * LANES, (s + 1) * LANES)
            y = (b_ref[0][s, pl.ds(seg, LRU_SEG, stride=LRU_SEGS), :]
                 + b_ref[1][s, pl.ds(seg, LRU_SEG, stride=LRU_SEGS), :])
            o_ref[pl.ds(t0, LRU_SEG), cols] = (g[:, cols] * y).astype(o_ref.dtype)
        return carry

    lax.fori_loop(0, LRU_SEGS, finish, 0)


def _lru_weights(wa, wx):
    def slab(w):
        w = 0.5 * w.reshape(DEPTH, LRU_SLABS, 2, LRU_BLOCK_DIM, LRU_BLOCK_DIM)
        z = jnp.zeros_like(w[:, :, 0])
        top = jnp.concatenate([w[:, :, 0], z], axis=-1)
        bot = jnp.concatenate([z, w[:, :, 1]], axis=-1)
        return jnp.concatenate([top, bot], axis=-2)
    return jnp.concatenate([slab(wa[:, 0]), slab(wx[:, 0]), slab(wa[:, 1]), slab(wx[:, 1])],
                           axis=-1).astype(BF16)


def _rglru(proj, conv_w, conv_b, w4, bias4, lam, layer):
    blk = P_LRU // LRU_WIDTH
    full = lambda *shape: _layer_spec(shape, lambda b: (layer,) + (0,) * len(shape))
    state = pltpu.VMEM((LRU_SLABS, SEQ, LANES), F32)
    return pl.pallas_call(
        _lru_kernel,
        grid=(BATCH,),
        in_specs=[
            pl.BlockSpec((SEQ, LRU_WIDTH), lambda b: (b, blk)),
            pl.BlockSpec((SEQ, LRU_WIDTH), lambda b: (b, blk + 1)),
            full(4, LRU_WIDTH), full(1, LRU_WIDTH),
            full(LRU_SLABS, LANES, 4 * LANES), full(4, LRU_WIDTH), full(2, LRU_WIDTH),
        ],
        out_specs=pl.BlockSpec((SEQ, LRU_WIDTH), lambda b: (b, 0)),
        out_shape=jax.ShapeDtypeStruct((TOKENS, LRU_WIDTH), BF16),
        scratch_shapes=[pltpu.VMEM((SEQ + 2 * LRU_PAD, LRU_WIDTH), F32), state, state, state, state],
        compiler_params=_cparams(("parallel",)),
        name="rglru",
    )(proj, proj, conv_w, conv_b, w4, bias4, lam)


TAIL_TM = 512
MLP_TF = 512
GATE_HALF = D_MODEL // 2


def _merged_branches(y_refs, gate_refs, bg_ref, wb_ref):
    branches = tuple(zip(y_refs, gate_refs))
    halves = []
    for half in range(D_MODEL // GATE_HALF):
        cols = slice(half * GATE_HALF, (half + 1) * GATE_HALF)
        merged = None
        for n, (y_ref, gates) in enumerate(branches):
            branch = jnp.dot(y_ref[...], wb_ref[n, :, cols], preferred_element_type=F32)
            gate = jax.nn.sigmoid(gates[half][...].astype(F32) + bg_ref[n:n + 1, cols])
            merged = gate * branch if merged is None else merged + gate * branch
        halves.append(merged.astype(BF16))
    return jnp.concatenate(halves, axis=1)


def _tail_kernel(x_ref, yn_ref, yr_ref, yl_ref, g00, g01, g10, g11, g20, g21, bg_ref, wb_ref, wo_ref,
                 g_ref, wu_ref, wd_ref, p_ref, gp_ref, wg_ref, wp_ref, gf_ref, o_ref, *, final):
    merged = _merged_branches((yn_ref, yr_ref, yl_ref), ((g00, g01), (g10, g11), (g20, g21)), bg_ref, wb_ref)
    x = x_ref[...] + jnp.dot(merged, wo_ref[...], preferred_element_type=F32)
    h = (x * _rms_scale(x) * g_ref[...]).astype(BF16)
    o_ref[...] = x
    for c in range(D_FF // MLP_TF):
        ff = slice(c * MLP_TF, (c + 1) * MLP_TF)
        u = jnp.maximum(jnp.dot(h, wu_ref[:, ff], preferred_element_type=F32), 0.0)
        o_ref[...] += jnp.dot((u * u).astype(BF16), wd_ref[ff, :], preferred_element_type=F32)
    x = o_ref[...]
    h = (x * _rms_scale(x) * gp_ref[...]).astype(BF16)
    gate = jax.nn.sigmoid(jnp.dot(h, wg_ref[...], preferred_element_type=F32))
    emb = jnp.dot(p_ref[...].astype(BF16), wp_ref[...], preferred_element_type=F32)
    x = x + gate * emb
    if final:
        x = x * _rms_scale(x) * gf_ref[...]
    o_ref[...] = x


def _tail(x, y_na, y_ret, y_lru, proj, b_gate, w_branch, w_out, g_mlp, w_up, w_down, p, g_ple, w_gate, w_ple,
          g_final, layer):
    xtile = pl.BlockSpec((TAIL_TM, D_MODEL), lambda i: (i, 0))
    ytile = pl.BlockSpec((TAIL_TM, NA_WIDTH), lambda i: (i, 0))
    gate = lambda c: pl.BlockSpec((TAIL_TM, GATE_HALF), lambda i: (i, P_GATE // GATE_HALF + c))
    vec = _layer_spec((1, D_MODEL), lambda i: (layer, 0, 0))
    resident = lambda *shape: pl.BlockSpec((None,) + shape, lambda i: (layer,) + (0,) * len(shape),
                                           pipeline_mode=pl.Buffered(1))
    final = layer == DEPTH - 1
    return pl.pallas_call(
        functools.partial(_tail_kernel, final=final),
        grid=(TOKENS // TAIL_TM,),
        in_specs=[
            xtile, ytile, ytile, ytile, gate(0), gate(1), gate(2), gate(3), gate(4), gate(5),
            _layer_spec((N_BRANCH, D_MODEL), lambda i: (layer, 0, 0)),
            resident(N_BRANCH, NA_WIDTH, D_MODEL),
            resident(D_MODEL, D_MODEL),
            vec,
            resident(D_MODEL, D_FF),
            resident(D_FF, D_MODEL),
            _layer_spec((TAIL_TM, PLE_DIM), lambda i: (layer, i, 0)),
            vec,
            resident(D_MODEL, D_MODEL),
            resident(PLE_DIM, D_MODEL),
            pl.BlockSpec((1, D_MODEL), lambda i: (0, 0)),
        ],
        out_specs=xtile,
        out_shape=jax.ShapeDtypeStruct((TOKENS, D_MODEL), F32),
        compiler_params=_cparams(("parallel",)),
        name="tail_final" if final else "tail",
    )(x, y_na, y_ret, y_lru, proj, proj, proj, proj, proj, proj, b_gate, w_branch, w_out,
      g_mlp, w_up, w_down, p, g_ple, w_gate, w_ple, g_final)


_IN_COL_SCALE = np.ones((IN_COLS,), np.float32)
_IN_COL_SCALE[P_NA:P_NA + NA_WIDTH] = NA_QSCALE


def kernel(x, p, g_mix, w_in, b_gate, na_rpb, ret_gn, conv_w, conv_b, lru_wa, lru_ba, lru_wx, lru_bx,
           lru_lambda, w_branch, w_out, g_mlp, w_up, w_down, g_ple, w_ple_gate, w_ple, g_final):
    rows = lambda v: v.reshape(DEPTH, 1, -1).astype(F32)
    w_in_b = (w_in * jnp.asarray(_IN_COL_SCALE)).astype(BF16)
    w_branch_b, w_out_b = w_branch.astype(BF16), w_out.astype(BF16)
    w_up_b, w_down_b = w_up.astype(BF16), w_down.astype(BF16)
    w_gate_b, w_ple_b = w_ple_gate.astype(BF16), w_ple.astype(BF16)
    g_mix_r, g_mlp_r, g_ple_r, gn_r, conv_b_r = rows(g_mix), rows(g_mlp), rows(g_ple), rows(ret_gn), rows(conv_b)
    rpb = _na_pad_rpb(na_rpb)
    lru_w4 = _lru_weights(lru_wa, lru_wx)
    lru_bias4 = 0.5 * jnp.stack([lru_ba[:, 0], lru_bx[:, 0], lru_ba[:, 1], lru_bx[:, 1]], axis=1).astype(F32)
    ret_tables = _ret_tables()
    pt = p.reshape(DEPTH, TOKENS, PLE_DIM)
    g_final_r = g_final.reshape(1, D_MODEL).astype(F32)

    xt = x.reshape(TOKENS, D_MODEL)
    for i in range(DEPTH):
        proj = _in_proj(xt, g_mix_r, w_in_b, ret_tables[0], ret_tables[1], i)
        y_na = _na(proj, rpb, i)
        y_ret = _retention(proj, gn_r, ret_tables, i)
        y_lru = _rglru(proj, conv_w.astype(F32), conv_b_r, lru_w4, lru_bias4, lru_lambda.astype(F32), i)
        xt = _tail(xt, y_na, y_ret, y_lru, proj, b_gate.astype(F32), w_branch_b, w_out_b, g_mlp_r, w_up_b,
                   w_down_b, pt, g_ple_r, w_gate_b, w_ple_b, g_final_r, i)
    return xt.reshape(BATCH, SEQ, D_MODEL)
```

```python
import functools
import math

import numpy as np
import jax
import jax.numpy as jnp
from jax import lax
from jax.experimental import pallas as pl
from jax.experimental.pallas import tpu as pltpu

F32 = jnp.float32
BF16 = jnp.bfloat16

D_MODEL = 1024
BATCH = 8
SEQ = 2048
DEPTH = 2
TOKENS = BATCH * SEQ

GRID_W = 64
GRID_H = SEQ // GRID_W
WIN_H = 8
WIN_W = 16
NA_HEADS = 8
NA_HEAD_DIM = 64
NA_WIDTH = NA_HEADS * NA_HEAD_DIM
RET_HEADS = 4
RET_HEAD_DIM = 128
RET_WIDTH = RET_HEADS * RET_HEAD_DIM
RET_THETA_BASE = 10000.0
LRU_WIDTH = 512
LRU_BLOCKS = 8
LRU_BLOCK_DIM = LRU_WIDTH // LRU_BLOCKS
LRU_C = 8.0
N_BRANCH = 3
D_FF = 4 * D_MODEL
PLE_DIM = 256
RMS_EPS = 1e-6

P_NA = 0
P_RET = 3 * NA_WIDTH
P_LRU = P_RET + 4 * RET_WIDTH
P_GATE = P_LRU + 2 * LRU_WIDTH
IN_COLS = P_GATE + N_BRANCH * D_MODEL

LANES = 128
SUBLANES = 8
VMEM_LIMIT = 56 * 1024 * 1024

RET_CHUNK = 256
MASK_NEG = -1e30
LOG2E = math.log2(math.e)


def _cparams(sem):
    return pltpu.CompilerParams(dimension_semantics=sem, vmem_limit_bytes=VMEM_LIMIT)


def _rms_scale(x):
    return lax.rsqrt(jnp.mean(x * x, axis=-1, keepdims=True) + RMS_EPS)


def _layer_spec(shape, index_map):
    return pl.BlockSpec((None,) + shape, index_map)


def _resident_spec(shape):
    return pl.BlockSpec(shape, lambda i: (0,) * len(shape), pipeline_mode=pl.Buffered(1))


def _cast_specs(weights, layer):
    ins = [pl.BlockSpec((None, w.shape[1] // BATCH, w.shape[2]), lambda b: (layer, b, 0)) for w in weights]
    outs = [pl.BlockSpec((w.shape[1] // BATCH, w.shape[2]), lambda b: (b, 0)) for w in weights]
    shapes = [jax.ShapeDtypeStruct(w.shape[1:], BF16) for w in weights]
    return ins, outs, shapes


def _cast_slabs(src_refs, dst_refs):
    for src, dst in zip(src_refs, dst_refs):
        dst[...] = src[...].astype(dst.dtype)


IN_TM = 512
IN_TN = 512


def _inproj_kernel(x_ref, g_ref, w_ref, cos_ref, sin_ref, o_ref):
    x = x_ref[...]
    h = (x * _rms_scale(x) * g_ref[...]).astype(BF16)
    lane = lax.broadcasted_iota(jnp.int32, (IN_TM, LANES), 1)
    even = (lane & 1) == 0
    cos = cos_ref[...]
    sin = sin_ref[...]

    def rotate(t, scale):
        heads = []
        for hd in range(IN_TN // RET_HEAD_DIM):
            th = t[:, hd * RET_HEAD_DIM:(hd + 1) * RET_HEAD_DIM]
            partner = jnp.where(even, pltpu.roll(th, LANES - 1, 1), pltpu.roll(th, 1, 1))
            th = th * cos + partner * sin
            heads.append(th if scale is None else th * scale)
        return jnp.concatenate(heads, axis=1)

    ret_q, ret_k, ret_g = P_RET // IN_TN, P_RET // IN_TN + 1, P_RET // IN_TN + 3
    for c in range(IN_COLS // IN_TN):
        cols = slice(c * IN_TN, (c + 1) * IN_TN)
        r = jnp.dot(h, w_ref[:, cols], preferred_element_type=F32)
        if c == ret_q:
            r = rotate(r, RET_HEAD_DIM ** -0.5)
        elif c == ret_k:
            r = rotate(r, None)
        elif c == ret_g:
            r = r * jax.nn.sigmoid(r)
        o_ref[:, cols] = r.astype(o_ref.dtype)


def _in_proj(x, g, w, cos, sin, layer):
    rot = pl.BlockSpec((IN_TM, RET_HEAD_DIM), lambda i: (i % (SEQ // IN_TM), 0))
    return pl.pallas_call(
        _inproj_kernel,
        grid=(TOKENS // IN_TM,),
        in_specs=[
            pl.BlockSpec((IN_TM, D_MODEL), lambda i: (i, 0)),
            _layer_spec((1, D_MODEL), lambda i: (layer, 0, 0)),
            _resident_spec((D_MODEL, IN_COLS)),
            rot, rot,
        ],
        out_specs=pl.BlockSpec((IN_TM, IN_COLS), lambda i: (i, 0)),
        out_shape=jax.ShapeDtypeStruct((TOKENS, IN_COLS), BF16),
        compiler_params=_cparams(("parallel",)),
        name="in_proj",
    )(x, g, w, cos, sin)


NA_PAIRS = NA_HEADS // 2
NA_KEYS = WIN_H * GRID_W
NA_BIAS_ROWS = 2 * WIN_H - 2
NA_QSCALE = NA_HEAD_DIM ** -0.5 * LOG2E
NA_ROWS_PER_STEP = 8


def _na_kernel(q_ref, k_ref, v_ref, rpb_ref, wsrc_ref, o_ref, wdst_ref, s_ref, t2_ref):
    _cast_slabs([wsrc_ref], [wdst_ref])
    lane = lax.broadcasted_iota(jnp.int32, (GRID_W, LANES), 1)
    lo = lane < NA_HEAD_DIM

    @pl.when(pl.program_id(0) == 0)
    def _():
        qcol = lax.broadcasted_iota(jnp.int32, (GRID_W, LANES), 0)
        kcol = jnp.where(lo, lane, lane - GRID_W)
        col_start = jnp.clip(qcol - WIN_W // 2, 0, GRID_W - WIN_W)
        off = kcol - col_start

        def build(ridx, carry):
            for h in range(NA_HEADS):
                xa = jnp.broadcast_to(rpb_ref[h, pl.ds(ridx, 1), :], (GRID_W, LANES))
                xb = jnp.broadcast_to(rpb_ref[h, pl.ds(ridx + 1, 1), :], (GRID_W, LANES))
                ta = pltpu.roll(xa, GRID_W + 1, 1, stride=1, stride_axis=0)
                tb = pltpu.roll(xb, 1, 1, stride=1, stride_axis=0)
                tile = jnp.where(off >= 0, jnp.where(off < WIN_W, jnp.where(lo, ta, tb), MASK_NEG), MASK_NEG)
                t2_ref[h // 2, ridx, (h % 2) * GRID_W:(h % 2 + 1) * GRID_W, :] = tile
            return carry

        lax.fori_loop(0, NA_BIAS_ROWS, build, 0)

    def window(r):
        if isinstance(r, int):
            r_start = min(max(r - WIN_H // 2, 0), GRID_H - WIN_H)
            return r_start * GRID_W, r_start - r + (WIN_H - 1)
        r_start = jnp.clip(r - WIN_H // 2, 0, GRID_H - WIN_H)
        return pl.multiple_of(r_start * GRID_W, GRID_W), r_start - r + (WIN_H - 1)

    def token_row(r):
        return r * GRID_W if isinstance(r, int) else pl.multiple_of(r * GRID_W, GRID_W)

    def scores(r, slot):
        kw0, d = window(r)
        q_r = q_ref[pl.ds(token_row(r), GRID_W), :]
        for p in range(NA_PAIRS):
            cols = slice(p * LANES, (p + 1) * LANES)
            qp = q_r[:, cols]
            zero = jnp.zeros_like(qp)
            q2 = jnp.concatenate([jnp.where(lo, qp, zero), jnp.where(lo, zero, qp)], axis=0)
            kp = k_ref[pl.ds(kw0, NA_KEYS), cols]
            s = lax.dot_general(q2, kp, (((1,), (1,)), ((), ())), preferred_element_type=F32)
            for c in range(NA_KEYS // LANES):
                kc = slice(c * LANES, (c + 1) * LANES)
                s_ref[slot, p, :, kc] = s[:, kc] + t2_ref[p, d + 2 * c]

    def attend(r, slot):
        kw0, _ = window(r)
        outs = []
        for p in range(NA_PAIRS):
            cols = slice(p * LANES, (p + 1) * LANES)
            s = s_ref[slot, p]
            e = jnp.exp2(s - jnp.max(s, axis=-1, keepdims=True))
            l = jnp.sum(e, axis=-1, keepdims=True)
            vp = v_ref[pl.ds(kw0, NA_KEYS), cols]
            o = jnp.dot(e.astype(BF16), vp, preferred_element_type=F32) / l
            outs.append(jnp.where(lo, o[:GRID_W], o[GRID_W:]))
        o_ref[pl.ds(token_row(r), GRID_W), :] = (
            jnp.concatenate(outs, axis=1).astype(o_ref.dtype))

    scores(0, 0)

    def rows(j, carry):
        r0 = NA_ROWS_PER_STEP * j
        for i in range(NA_ROWS_PER_STEP):
            scores(jnp.minimum(r0 + i + 1, GRID_H - 1), (i + 1) % 2)
            attend(r0 + i, i % 2)
        return carry

    lax.fori_loop(0, GRID_H // NA_ROWS_PER_STEP, rows, 0)


NA_RPB_ROWS = 2 * WIN_H
NA_RPB_LANE0 = GRID_W - WIN_W


def _na_pad_rpb(rpb):
    pad = ((0, 0), (0, 0), (0, NA_RPB_ROWS - rpb.shape[2]),
           (NA_RPB_LANE0, LANES - rpb.shape[3] - NA_RPB_LANE0))
    return jnp.pad(rpb.astype(F32) * LOG2E, pad)


def _na(proj, rpb, w_f32, layer):
    blk = P_NA // NA_WIDTH
    cast_in, cast_out, cast_shape = _cast_specs([w_f32], layer)
    return pl.pallas_call(
        _na_kernel,
        grid=(BATCH,),
        in_specs=[
            pl.BlockSpec((SEQ, NA_WIDTH), lambda b: (b, blk)),
            pl.BlockSpec((SEQ, NA_WIDTH), lambda b: (b, blk + 1)),
            pl.BlockSpec((SEQ, NA_WIDTH), lambda b: (b, blk + 2)),
            _layer_spec((NA_HEADS, NA_RPB_ROWS, LANES), lambda b: (layer, 0, 0, 0)),
        ] + cast_in,
        out_specs=[pl.BlockSpec((SEQ, NA_WIDTH), lambda b: (b, 0))] + cast_out,
        out_shape=[jax.ShapeDtypeStruct((TOKENS, NA_WIDTH), BF16)] + cast_shape,
        scratch_shapes=[pltpu.VMEM((2, NA_PAIRS, 2 * GRID_W, NA_KEYS), F32),
                        pltpu.VMEM((NA_PAIRS, NA_BIAS_ROWS, 2 * GRID_W, LANES), F32)],
        compiler_params=_cparams(("arbitrary",)),
        name="na",
    )(proj, proj, proj, rpb, w_f32)


RET_NCHUNK = SEQ // RET_CHUNK


RET_NCAST = 4


def _ret_kernel(q_ref, k_ref, v_ref, g_ref, dmat_ref, dec_ref, cdec_ref, gn_ref, *refs, next_w_in):
    n_in = RET_NCAST + (2 if next_w_in else 0)
    o_ref, st_ref = refs[n_in], refs[-1]
    _cast_slabs(refs[:RET_NCAST], refs[n_in + 1:n_in + 1 + RET_NCAST])
    if next_w_in:
        win_ref, scale_ref, win_out_ref = refs[RET_NCAST], refs[RET_NCAST + 1], refs[n_in + 1 + RET_NCAST]
        win_out_ref[...] = (win_ref[...] * scale_ref[...]).astype(win_out_ref.dtype)

    def decayed(t, dec):
        return (t.astype(F32) * dec).astype(BF16)

    chunk = lambda c: slice(c * RET_CHUNK, (c + 1) * RET_CHUNK)
    last = RET_NCHUNK - 1

    for h in range(RET_HEADS):
        hd = slice(h * RET_HEAD_DIM, (h + 1) * RET_HEAD_DIM)
        dmat = dmat_ref[h]
        q_f, k_f, q_b, k_b = dec_ref[h, 0], dec_ref[h, 1], dec_ref[h, 2], dec_ref[h, 3]
        c_f, c_b = cdec_ref[h, 0:1, :], cdec_ref[h, 1:2, :]

        def summary(c, kdec):
            return lax.dot_general(decayed(k_ref[chunk(c), hd], kdec), v_ref[chunk(c), hd],
                                   (((0,), (0,)), ((), ())), preferred_element_type=F32)

        kv_f = [summary(c, k_f) for c in range(last)]
        kv_b = [None] + [summary(c, k_b) for c in range(1, RET_NCHUNK)]

        state = None
        for c in range(1, RET_NCHUNK):
            state = kv_f[c - 1] if state is None else state * c_f + kv_f[c - 1]
            st_ref[h, 0, c] = state.astype(BF16)
        state = None
        for c in range(last - 1, -1, -1):
            state = kv_b[c + 1] if state is None else state * c_b + kv_b[c + 1]
            st_ref[h, 1, c] = state.astype(BF16)

        gn = gn_ref[:, hd]
        for c in range(RET_NCHUNK):
            qc, kc, vc = q_ref[chunk(c), hd], k_ref[chunk(c), hd], v_ref[chunk(c), hd]
            s = lax.dot_general(qc, kc, (((1,), (1,)), ((), ())), preferred_element_type=F32) * dmat
            y = jnp.dot(s.astype(BF16), vc, preferred_element_type=F32)
            if c > 0:
                y = y + jnp.dot(decayed(qc, q_f), st_ref[h, 0, c], preferred_element_type=F32)
            if c < last:
                y = y + jnp.dot(decayed(qc, q_b), st_ref[h, 1, c], preferred_element_type=F32)
            y = y * _rms_scale(y) * gn
            o_ref[chunk(c), hd] = (g_ref[chunk(c), hd].astype(F32) * y).astype(o_ref.dtype)


def _ret_tables():
    pos = np.arange(SEQ, dtype=np.float64)
    theta = 1.0 / (RET_THETA_BASE ** np.linspace(0.0, 1.0, RET_HEAD_DIM // 2))
    ang = pos[:, None] * theta[None, :]
    sign = np.tile(np.array([-1.0, 1.0]), RET_HEAD_DIM // 2)
    cos = np.repeat(np.cos(ang), 2, axis=1)
    sin = np.repeat(np.sin(ang), 2, axis=1) * sign
    hidx = np.arange(RET_HEADS, dtype=np.float64)
    lg_f = np.log1p(-np.exp2(-5.0 - hidx))[:, None]
    lg_b = np.log1p(-np.exp2(-5.5 - hidx))[:, None]
    idx = np.arange(RET_CHUNK, dtype=np.float64)
    diff = idx[:, None] - idx[None, :]
    dmat = np.where(diff >= 0, np.exp(lg_f[:, :, None] * np.maximum(diff, 0.0)),
                    np.exp(lg_b[:, :, None] * np.maximum(-diff, 0.0)))
    dec = np.stack([np.exp(lg_f * (idx + 1.0)), np.exp(lg_f * (RET_CHUNK - 1.0 - idx)),
                    np.exp(lg_b * (RET_CHUNK - idx)), np.exp(lg_b * idx)], axis=1)
    dec = np.broadcast_to(dec[..., None], (RET_HEADS, 4, RET_CHUNK, LANES))
    cdec = np.stack([np.exp(lg_f * RET_CHUNK), np.exp(lg_b * RET_CHUNK)], axis=1)
    cdec = np.broadcast_to(cdec, (RET_HEADS, 2, LANES))
    return tuple(jnp.asarray(np.ascontiguousarray(t), F32) for t in (cos, sin, dmat, dec, cdec))


def _retention(proj, gn, tables, side_weights, w_in, in_scale, layer):
    _, _, dmat, dec, cdec = tables
    blk = P_RET // RET_WIDTH
    tok = lambda off: pl.BlockSpec((SEQ, RET_WIDTH), lambda b: (b, blk + off))
    cast_in, cast_out, cast_shape = _cast_specs(side_weights, layer)
    next_w_in = layer + 1 < DEPTH
    if next_w_in:
        win_in, win_out, win_shape = _cast_specs([w_in], layer + 1)
        cast_in = cast_in + win_in + [pl.BlockSpec((1, IN_COLS), lambda b: (0, 0))]
        cast_out, cast_shape = cast_out + win_out, cast_shape + win_shape
    return pl.pallas_call(
        functools.partial(_ret_kernel, next_w_in=next_w_in),
        grid=(BATCH,),
        in_specs=[
            tok(0), tok(1), tok(2), tok(3),
            pl.BlockSpec((RET_HEADS, RET_CHUNK, RET_CHUNK), lambda b: (0, 0, 0)),
            pl.BlockSpec((RET_HEADS, 4, RET_CHUNK, LANES), lambda b: (0, 0, 0, 0)),
            pl.BlockSpec((RET_HEADS, 2, LANES), lambda b: (0, 0, 0)),
            _layer_spec((1, RET_WIDTH), lambda b: (layer, 0, 0)),
        ] + cast_in,
        out_specs=[pl.BlockSpec((SEQ, RET_WIDTH), lambda b: (b, 0))] + cast_out,
        out_shape=[jax.ShapeDtypeStruct((TOKENS, RET_WIDTH), BF16)] + cast_shape,
        scratch_shapes=[pltpu.VMEM((RET_HEADS, 2, RET_NCHUNK, RET_HEAD_DIM, RET_HEAD_DIM), BF16)],
        compiler_params=_cparams(("parallel",)),
        name="retention",
    )(proj, proj, proj, proj, dmat, dec, cdec, gn, *side_weights, *((w_in, in_scale) if next_w_in else ()))


LRU_SEGS = SUBLANES
LRU_SEG = SEQ // LRU_SEGS
LRU_PAD = SUBLANES
LRU_SLABS = LRU_WIDTH // LANES


def _softplus(x):
    return jnp.maximum(x, 0.0) + jnp.log1p(jnp.exp(-jnp.abs(x)))


def _lru_kernel(xc_ref, gc_ref, cw_ref, cb_ref, w_ref, bias_ref, lam_ref, wsrc_ref, o_ref, wdst_ref,
                xpad_ref, af_ref, ab_ref, bf_ref, bb_ref):
    _cast_slabs([wsrc_ref], [wdst_ref])
    a_ref, b_ref = (af_ref, ab_ref), (bf_ref, bb_ref)
    zeros = jnp.zeros((LRU_PAD, LRU_WIDTH), F32)
    xpad_ref[0:LRU_PAD, :] = zeros
    xpad_ref[LRU_PAD + SEQ:, :] = zeros
    xpad_ref[LRU_PAD:LRU_PAD + SEQ, :] = xc_ref[...].astype(F32)

    cw = cw_ref[...]
    cb = cb_ref[...]
    half_k = (-0.5 * LRU_C * LOG2E) * _softplus(-lam_ref[...])
    row = lax.broadcasted_iota(jnp.int32, (LRU_SEG, LANES), 0)

    def gates(seg, first_dir):
        t0 = pl.multiple_of(seg * LRU_SEG, LRU_SEG)
        xw = xpad_ref[pl.ds(t0, LRU_SEG + 2 * LRU_PAD), :]
        taps = [xw[LRU_PAD - 2 + j:LRU_PAD - 2 + j + LRU_SEG, :] for j in range(4)]
        xf = taps[0] * cw[0:1] + taps[1] * cw[1:2] + taps[2] * cw[2:3] + taps[3] * cw[3:4] + cb
        for s in range(LRU_SLABS):
            cols = slice(s * LANES, (s + 1) * LANES)
            xs = xf[:, cols]
            xh = 0.5 * xs
            z = jnp.dot(xs.astype(BF16), w_ref[s], preferred_element_type=F32)
            for dr in range(2):
                zr = z[:, (2 * dr) * LANES:(2 * dr + 1) * LANES] + bias_ref[2 * dr:2 * dr + 1, cols]
                zi = z[:, (2 * dr + 1) * LANES:(2 * dr + 2) * LANES] + bias_ref[2 * dr + 1:2 * dr + 2, cols]
                hk = half_k[dr:dr + 1, cols]
                a = jnp.exp2(jnp.tanh(zr) * hk + hk)
                if first_dir == dr:
                    a = jnp.where(row == (0 if dr == 0 else LRU_SEG - 1), 0.0, a)
                om = 1.0 - a * a
                mult = om * lax.rsqrt(jnp.maximum(om, 1e-30))
                a_ref[dr][s, pl.ds(seg, LRU_SEG, stride=LRU_SEGS), :] = a
                b_ref[dr][s, pl.ds(seg, LRU_SEG, stride=LRU_SEGS), :] = mult * (jnp.tanh(zi) * xh + xh)

    gates(0, 0)
    lax.fori_loop(1, LRU_SEGS - 1, lambda seg, c: (gates(seg, None), c)[1], 0)
    gates(LRU_SEGS - 1, 1)

    def group(dr, j):
        return pl.ds(pl.multiple_of((j if dr == 0 else LRU_SEG - 1 - j) * LRU_SEGS, LRU_SEGS), LRU_SEGS)

    def probe(j, carry):
        new = []
        for dr in range(2):
            for s in range(LRU_SLABS):
                h, p = carry[dr * LRU_SLABS + s]
                a = a_ref[dr][s, group(dr, j), :]
                new.append((a * h + b_ref[dr][s, group(dr, j), :], a * p))
        return tuple(new)

    init = (jnp.zeros((LRU_SEGS, LANES), F32), jnp.ones((LRU_SEGS, LANES), F32))
    ends = lax.fori_loop(0, LRU_SEG, probe, (init,) * (2 * LRU_SLABS), unroll=2)

    def carry_in(dr, s):
        h_end, p_end = ends[dr * LRU_SLABS + s]
        state = jnp.zeros((1, LANES), F32)
        rows_in = [None] * LRU_SEGS
        for seg in (range(LRU_SEGS) if dr == 0 else range(LRU_SEGS - 1, -1, -1)):
            rows_in[seg] = state
            state = h_end[seg:seg + 1, :] + p_end[seg:seg + 1, :] * state
        return jnp.concatenate(rows_in, axis=0)

    def scan(j, hs):
        new = []
        for dr in range(2):
            for s in range(LRU_SLABS):
                h = a_ref[dr][s, group(dr, j), :] * hs[dr * LRU_SLABS + s] + b_ref[dr][s, group(dr, j), :]
                b_ref[dr][s, group(dr, j), :] = h
                new.append(h)
        return tuple(new)

    lax.fori_loop(0, LRU_SEG, scan, tuple(carry_in(dr, s) for dr in range(2) for s in range(LRU_SLABS)),
                  unroll=2)

    def finish(seg, carry):
        t0 = pl.multiple_of(seg * LRU_SEG, LRU_SEG)
        g = jax.nn.gelu(gc_ref[pl.ds(t0, LRU_SEG), :].astype(F32), approximate=True)
        for s in range(LRU_SLABS):
            cols = slice(s * LANES, (s + 1) * LANES)
            y = (b_ref[0][s, pl.ds(seg, LRU_SEG, stride=LRU_SEGS), :]
                 + b_ref[1][s, pl.ds(seg, LRU_SEG, stride=LRU_SEGS), :])
            o_ref[pl.ds(t0, LRU_SEG), cols] = (g[:, cols] * y).astype(o_ref.dtype)
        return carry

    lax.fori_loop(0, LRU_SEGS, finish, 0)


def _lru_weights(wa, wx):
    def slab(w):
        w = 0.5 * w.reshape(DEPTH, LRU_SLABS, 2, LRU_BLOCK_DIM, LRU_BLOCK_DIM)
        z = jnp.zeros_like(w[:, :, 0])
        top = jnp.concatenate([w[:, :, 0], z], axis=-1)
        bot = jnp.concatenate([z, w[:, :, 1]], axis=-1)
        return jnp.concatenate([top, bot], axis=-2)
    return jnp.concatenate([slab(wa[:, 0]), slab(wx[:, 0]), slab(wa[:, 1]), slab(wx[:, 1])],
                           axis=-1).astype(BF16)


def _rglru(proj, conv_w, conv_b, w4, bias4, lam, w_f32, layer):
    blk = P_LRU // LRU_WIDTH
    full = lambda *shape: _layer_spec(shape, lambda b: (layer,) + (0,) * len(shape))
    state = pltpu.VMEM((LRU_SLABS, SEQ, LANES), F32)
    cast_in, cast_out, cast_shape = _cast_specs([w_f32], layer)
    return pl.pallas_call(
        _lru_kernel,
        grid=(BATCH,),
        in_specs=[
            pl.BlockSpec((SEQ, LRU_WIDTH), lambda b: (b, blk)),
            pl.BlockSpec((SEQ, LRU_WIDTH), lambda b: (b, blk + 1)),
            full(4, LRU_WIDTH), full(1, LRU_WIDTH),
            full(LRU_SLABS, LANES, 4 * LANES), full(4, LRU_WIDTH), full(2, LRU_WIDTH),
        ] + cast_in,
        out_specs=[pl.BlockSpec((SEQ, LRU_WIDTH), lambda b: (b, 0))] + cast_out,
        out_shape=[jax.ShapeDtypeStruct((TOKENS, LRU_WIDTH), BF16)] + cast_shape,
        scratch_shapes=[pltpu.VMEM((SEQ + 2 * LRU_PAD, LRU_WIDTH), F32), state, state, state, state],
        compiler_params=_cparams(("parallel",)),
        name="rglru",
    )(proj, proj, conv_w, conv_b, w4, bias4, lam, w_f32)


TAIL_TM = 512
MLP_TF = 512
GATE_HALF = D_MODEL // 2


def _merged_branches(y_refs, gate_refs, bg_ref, wb_ref):
    branches = tuple(zip(y_refs, gate_refs))
    halves = []
    for half in range(D_MODEL // GATE_HALF):
        cols = slice(half * GATE_HALF, (half + 1) * GATE_HALF)
        merged = None
        for n, (y_ref, gates) in enumerate(branches):
            branch = jnp.dot(y_ref[...], wb_ref[n * NA_WIDTH:(n + 1) * NA_WIDTH, cols],
                             preferred_element_type=F32)
            gate = jax.nn.sigmoid(gates[half][...].astype(F32) + bg_ref[n:n + 1, cols])
            merged = gate * branch if merged is None else merged + gate * branch
        halves.append(merged.astype(BF16))
    return jnp.concatenate(halves, axis=1)


def _tail_kernel(x_ref, yn_ref, yr_ref, yl_ref, g00, g01, g10, g11, g20, g21, bg_ref, wb_ref, wo_ref,
                 g_ref, wu_ref, wd_ref, p_ref, gp_ref, wg_ref, wp_ref, gf_ref, o_ref, *, final):
    merged = _merged_branches((yn_ref, yr_ref, yl_ref), ((g00, g01), (g10, g11), (g20, g21)), bg_ref, wb_ref)
    x = x_ref[...] + jnp.dot(merged, wo_ref[...], preferred_element_type=F32)
    h = (x * _rms_scale(x) * g_ref[...]).astype(BF16)
    o_ref[...] = x
    for c in range(D_FF // MLP_TF):
        ff = slice(c * MLP_TF, (c + 1) * MLP_TF)
        u = jnp.maximum(jnp.dot(h, wu_ref[:, ff], preferred_element_type=F32), 0.0)
        o_ref[...] += jnp.dot((u * u).astype(BF16), wd_ref[ff, :], preferred_element_type=F32)
    x = o_ref[...]
    h = (x * _rms_scale(x) * gp_ref[...]).astype(BF16)
    gate = jax.nn.sigmoid(jnp.dot(h, wg_ref[...], preferred_element_type=F32))
    emb = jnp.dot(p_ref[...].astype(BF16), wp_ref[...], preferred_element_type=F32)
    x = x + gate * emb
    if final:
        x = x * _rms_scale(x) * gf_ref[...]
    o_ref[...] = x


def _tail(x, y_na, y_ret, y_lru, proj, b_gate, w_branch, w_out, g_mlp, w_up, w_down, p, g_ple, w_gate, w_ple,
          g_final, layer):
    xtile = pl.BlockSpec((TAIL_TM, D_MODEL), lambda i: (i, 0))
    ytile = pl.BlockSpec((TAIL_TM, NA_WIDTH), lambda i: (i, 0))
    gate = lambda c: pl.BlockSpec((TAIL_TM, GATE_HALF), lambda i: (i, P_GATE // GATE_HALF + c))
    vec = _layer_spec((1, D_MODEL), lambda i: (layer, 0, 0))
    resident = lambda *shape: _resident_spec(shape)
    final = layer == DEPTH - 1
    return pl.pallas_call(
        functools.partial(_tail_kernel, final=final),
        grid=(TOKENS // TAIL_TM,),
        in_specs=[
            xtile, ytile, ytile, ytile, gate(0), gate(1), gate(2), gate(3), gate(4), gate(5),
            _layer_spec((N_BRANCH, D_MODEL), lambda i: (layer, 0, 0)),
            resident(N_BRANCH * NA_WIDTH, D_MODEL),
            resident(D_MODEL, D_MODEL),
            vec,
            resident(D_MODEL, D_FF),
            resident(D_FF, D_MODEL),
            _layer_spec((TAIL_TM, PLE_DIM), lambda i: (layer, i, 0)),
            vec,
            resident(D_MODEL, D_MODEL),
            resident(PLE_DIM, D_MODEL),
            pl.BlockSpec((1, D_MODEL), lambda i: (0, 0)),
        ],
        out_specs=xtile,
        out_shape=jax.ShapeDtypeStruct((TOKENS, D_MODEL), F32),
        compiler_params=_cparams(("parallel",)),
        name="tail_final" if final else "tail",
    )(x, y_na, y_ret, y_lru, proj, proj, proj, proj, proj, proj, b_gate, w_branch, w_out,
      g_mlp, w_up, w_down, p, g_ple, w_gate, w_ple, g_final)


_IN_COL_SCALE = np.ones((IN_COLS,), np.float32)
_IN_COL_SCALE[P_NA:P_NA + NA_WIDTH] = NA_QSCALE


def kernel(x, p, g_mix, w_in, b_gate, na_rpb, ret_gn, conv_w, conv_b, lru_wa, lru_ba, lru_wx, lru_bx,
           lru_lambda, w_branch, w_out, g_mlp, w_up, w_down, g_ple, w_ple_gate, w_ple, g_final):
    rows = lambda v: v.reshape(DEPTH, 1, -1).astype(F32)
    in_scale = jnp.asarray(_IN_COL_SCALE).reshape(1, IN_COLS)
    w_in_b = (w_in[0] * in_scale).astype(BF16)
    side_weights = (w_branch.reshape(DEPTH, N_BRANCH * NA_WIDTH, D_MODEL), w_out, w_ple_gate, w_ple)
    g_mix_r, g_mlp_r, g_ple_r, gn_r, conv_b_r = rows(g_mix), rows(g_mlp), rows(g_ple), rows(ret_gn), rows(conv_b)
    rpb = _na_pad_rpb(na_rpb)
    lru_w4 = _lru_weights(lru_wa, lru_wx)
    lru_bias4 = 0.5 * jnp.stack([lru_ba[:, 0], lru_bx[:, 0], lru_ba[:, 1], lru_bx[:, 1]], axis=1).astype(F32)
    ret_tables = _ret_tables()
    pt = p.reshape(DEPTH, TOKENS, PLE_DIM)
    g_final_r = g_final.reshape(1, D_MODEL).astype(F32)

    xt = x.reshape(TOKENS, D_MODEL)
    for i in range(DEPTH):
        proj = _in_proj(xt, g_mix_r, w_in_b, ret_tables[0], ret_tables[1], i)
        y_na, w_up_b = _na(proj, rpb, w_up, i)
        y_ret, w_branch_b, w_out_b, w_gate_b, w_ple_b, *next_w_in = _retention(
            proj, gn_r, ret_tables, side_weights, w_in, in_scale, i)
        y_lru, w_down_b = _rglru(proj, conv_w.astype(F32), conv_b_r, lru_w4, lru_bias4,
                                 lru_lambda.astype(F32), w_down, i)
        xt = _tail(xt, y_na, y_ret, y_lru, proj, b_gate.astype(F32), w_branch_b, w_out_b, g_mlp_r, w_up_b,
                   w_down_b, pt, g_ple_r, w_gate_b, w_ple_b, g_final_r, i)
        if next_w_in:
            w_in_b = next_w_in[0]
    return xt.reshape(BATCH, SEQ, D_MODEL)
```

```python
import functools
import math

import numpy as np
import jax
import jax.numpy as jnp
from jax import lax
from jax.experimental import pallas as pl
from jax.experimental.pallas import tpu as pltpu

F32 = jnp.float32
BF16 = jnp.bfloat16

D_MODEL = 1024
BATCH = 8
SEQ = 2048
DEPTH = 2
TOKENS = BATCH * SEQ

GRID_W = 64
GRID_H = SEQ // GRID_W
WIN_H = 8
WIN_W = 16
NA_HEADS = 8
NA_HEAD_DIM = 64
NA_WIDTH = NA_HEADS * NA_HEAD_DIM
RET_HEADS = 4
RET_HEAD_DIM = 128
RET_WIDTH = RET_HEADS * RET_HEAD_DIM
RET_THETA_BASE = 10000.0
LRU_WIDTH = 512
LRU_BLOCKS = 8
LRU_BLOCK_DIM = LRU_WIDTH // LRU_BLOCKS
LRU_C = 8.0
N_BRANCH = 3
D_FF = 4 * D_MODEL
PLE_DIM = 256
RMS_EPS = 1e-6

W_NA = 0
W_RET = 3 * NA_WIDTH
W_LRU = W_RET + 4 * RET_WIDTH
W_GATE = W_LRU + 2 * LRU_WIDTH
IN_COLS = W_GATE + N_BRANCH * D_MODEL
P_GATE = 0
P_NA = N_BRANCH * D_MODEL
P_RET = P_NA + W_RET - W_NA
P_LRU = P_NA + W_LRU - W_NA

LANES = 128
SUBLANES = 8
VMEM_LIMIT = 56 * 1024 * 1024

RET_CHUNK = 256
MASK_NEG = -1e30
LOG2E = math.log2(math.e)


def _cparams(sem):
    return pltpu.CompilerParams(dimension_semantics=sem, vmem_limit_bytes=VMEM_LIMIT)


def _rms_scale(x):
    return lax.rsqrt(jnp.mean(x * x, axis=-1, keepdims=True) + RMS_EPS)


def _layer_spec(shape, index_map):
    return pl.BlockSpec((None,) + shape, index_map)


def _resident_spec(shape):
    return pl.BlockSpec(shape, lambda i: (0,) * len(shape), pipeline_mode=pl.Buffered(1))


def _cast_specs(weights, layer):
    ins = [pl.BlockSpec((None, w.shape[1] // BATCH, w.shape[2]), lambda b: (layer, b, 0)) for w in weights]
    outs = [pl.BlockSpec((w.shape[1] // BATCH, w.shape[2]), lambda b: (b, 0)) for w in weights]
    shapes = [jax.ShapeDtypeStruct(w.shape[1:], BF16) for w in weights]
    return ins, outs, shapes


def _cast_slabs(src_refs, dst_refs):
    for src, dst in zip(src_refs, dst_refs):
        dst[...] = src[...].astype(dst.dtype)


IN_TM = 512
IN_TN = 512


def _inproj_kernel(x_ref, g_ref, w_ref, cos_ref, sin_ref, o_ref):
    x = x_ref[...]
    h = (x * _rms_scale(x) * g_ref[...]).astype(BF16)
    lane = lax.broadcasted_iota(jnp.int32, (IN_TM, LANES), 1)
    even = (lane & 1) == 0
    cos = cos_ref[...]
    sin = sin_ref[...]

    def rotate(t, scale):
        heads = []
        for hd in range(IN_TN // RET_HEAD_DIM):
            th = t[:, hd * RET_HEAD_DIM:(hd + 1) * RET_HEAD_DIM]
            partner = jnp.where(even, pltpu.roll(th, LANES - 1, 1), pltpu.roll(th, 1, 1))
            th = th * cos + partner * sin
            heads.append(th if scale is None else th * scale)
        return jnp.concatenate(heads, axis=1)

    ret_q, ret_k, ret_g = P_RET // IN_TN, P_RET // IN_TN + 1, P_RET // IN_TN + 3
    for c in range(IN_COLS // IN_TN):
        cols = slice(c * IN_TN, (c + 1) * IN_TN)
        src = c * IN_TN + (W_GATE - P_GATE if c * IN_TN < P_NA else W_NA - P_NA)
        r = jnp.dot(h, w_ref[:, src:src + IN_TN], preferred_element_type=F32)
        if c == ret_q:
            r = rotate(r, RET_HEAD_DIM ** -0.5)
        elif c == ret_k:
            r = rotate(r, None)
        elif c == ret_g:
            r = r * jax.nn.sigmoid(r)
        o_ref[:, cols] = r.astype(o_ref.dtype)


def _in_proj(x, g, w, cos, sin, layer):
    rot = pl.BlockSpec((IN_TM, RET_HEAD_DIM), lambda i: (i % (SEQ // IN_TM), 0))
    return pl.pallas_call(
        _inproj_kernel,
        grid=(TOKENS // IN_TM,),
        in_specs=[
            pl.BlockSpec((IN_TM, D_MODEL), lambda i: (i, 0)),
            _layer_spec((1, D_MODEL), lambda i: (layer, 0, 0)),
            _resident_spec((D_MODEL, IN_COLS)),
            rot, rot,
        ],
        out_specs=pl.BlockSpec((IN_TM, IN_COLS), lambda i: (i, 0)),
        out_shape=jax.ShapeDtypeStruct((TOKENS, IN_COLS), BF16),
        compiler_params=_cparams(("parallel",)),
        name="in_proj",
    )(x, g, w, cos, sin)


NA_PAIRS = NA_HEADS // 2
NA_KEYS = WIN_H * GRID_W
NA_BIAS_ROWS = 2 * WIN_H - 2
NA_QSCALE = NA_HEAD_DIM ** -0.5 * LOG2E
NA_ROWS_PER_STEP = 8


def _na_kernel(q_ref, k_ref, v_ref, rpb_ref, wsrc_ref, o_ref, wdst_ref, s_ref, t2_ref):
    _cast_slabs([wsrc_ref], [wdst_ref])
    lane = lax.broadcasted_iota(jnp.int32, (GRID_W, LANES), 1)
    lo = lane < NA_HEAD_DIM

    @pl.when(pl.program_id(0) == 0)
    def _():
        qcol = lax.broadcasted_iota(jnp.int32, (GRID_W, LANES), 0)
        kcol = jnp.where(lo, lane, lane - GRID_W)
        col_start = jnp.clip(qcol - WIN_W // 2, 0, GRID_W - WIN_W)
        off = kcol - col_start

        def build(ridx, carry):
            for h in range(NA_HEADS):
                xa = jnp.broadcast_to(rpb_ref[h, pl.ds(ridx, 1), :], (GRID_W, LANES))
                xb = jnp.broadcast_to(rpb_ref[h, pl.ds(ridx + 1, 1), :], (GRID_W, LANES))
                ta = pltpu.roll(xa, GRID_W + 1, 1, stride=1, stride_axis=0)
                tb = pltpu.roll(xb, 1, 1, stride=1, stride_axis=0)
                tile = jnp.where(off >= 0, jnp.where(off < WIN_W, jnp.where(lo, ta, tb), MASK_NEG), MASK_NEG)
                t2_ref[h // 2, ridx, (h % 2) * GRID_W:(h % 2 + 1) * GRID_W, :] = tile
            return carry

        lax.fori_loop(0, NA_BIAS_ROWS, build, 0)

    def window(r):
        if isinstance(r, int):
            r_start = min(max(r - WIN_H // 2, 0), GRID_H - WIN_H)
            return r_start * GRID_W, r_start - r + (WIN_H - 1)
        r_start = jnp.clip(r - WIN_H // 2, 0, GRID_H - WIN_H)
        return pl.multiple_of(r_start * GRID_W, GRID_W), r_start - r + (WIN_H - 1)

    def token_row(r):
        return r * GRID_W if isinstance(r, int) else pl.multiple_of(r * GRID_W, GRID_W)

    def scores(r, slot):
        kw0, d = window(r)
        q_r = q_ref[pl.ds(token_row(r), GRID_W), :]
        for p in range(NA_PAIRS):
            cols = slice(p * LANES, (p + 1) * LANES)
            qp = q_r[:, cols]
            zero = jnp.zeros_like(qp)
            q2 = jnp.concatenate([jnp.where(lo, qp, zero), jnp.where(lo, zero, qp)], axis=0)
            kp = k_ref[pl.ds(kw0, NA_KEYS), cols]
            s = lax.dot_general(q2, kp, (((1,), (1,)), ((), ())), preferred_element_type=F32)
            for c in range(NA_KEYS // LANES):
                kc = slice(c * LANES, (c + 1) * LANES)
                s_ref[slot, p, :, kc] = s[:, kc] + t2_ref[p, d + 2 * c]

    def attend(r, slot):
        kw0, _ = window(r)
        outs = []
        for p in range(NA_PAIRS):
            cols = slice(p * LANES, (p + 1) * LANES)
            s = s_ref[slot, p]
            e = jnp.exp2(s - jnp.max(s, axis=-1, keepdims=True))
            l = jnp.sum(e, axis=-1, keepdims=True)
            vp = v_ref[pl.ds(kw0, NA_KEYS), cols]
            o = jnp.dot(e.astype(BF16), vp, preferred_element_type=F32) / l
            outs.append(jnp.where(lo, o[:GRID_W], o[GRID_W:]))
        o_ref[pl.ds(token_row(r), GRID_W), :] = (
            jnp.concatenate(outs, axis=1).astype(o_ref.dtype))

    scores(0, 0)

    def rows(j, carry):
        r0 = NA_ROWS_PER_STEP * j
        for i in range(NA_ROWS_PER_STEP):
            scores(jnp.minimum(r0 + i + 1, GRID_H - 1), (i + 1) % 2)
            attend(r0 + i, i % 2)
        return carry

    lax.fori_loop(0, GRID_H // NA_ROWS_PER_STEP, rows, 0)


NA_RPB_ROWS = 2 * WIN_H
NA_RPB_LANE0 = GRID_W - WIN_W


def _na_pad_rpb(rpb):
    pad = ((0, 0), (0, 0), (0, NA_RPB_ROWS - rpb.shape[2]),
           (NA_RPB_LANE0, LANES - rpb.shape[3] - NA_RPB_LANE0))
    return jnp.pad(rpb.astype(F32) * LOG2E, pad)


def _na(proj, rpb, w_f32, layer):
    blk = P_NA // NA_WIDTH
    cast_in, cast_out, cast_shape = _cast_specs([w_f32], layer)
    return pl.pallas_call(
        _na_kernel,
        grid=(BATCH,),
        in_specs=[
            pl.BlockSpec((SEQ, NA_WIDTH), lambda b: (b, blk)),
            pl.BlockSpec((SEQ, NA_WIDTH), lambda b: (b, blk + 1)),
            pl.BlockSpec((SEQ, NA_WIDTH), lambda b: (b, blk + 2)),
            _layer_spec((NA_HEADS, NA_RPB_ROWS, LANES), lambda b: (layer, 0, 0, 0)),
        ] + cast_in,
        out_specs=[pl.BlockSpec((SEQ, NA_WIDTH), lambda b: (b, 0))] + cast_out,
        out_shape=[jax.ShapeDtypeStruct((TOKENS, NA_WIDTH), BF16)] + cast_shape,
        scratch_shapes=[pltpu.VMEM((2, NA_PAIRS, 2 * GRID_W, NA_KEYS), F32),
                        pltpu.VMEM((NA_PAIRS, NA_BIAS_ROWS, 2 * GRID_W, LANES), F32)],
        compiler_params=_cparams(("arbitrary",)),
        name="na",
    )(proj, proj, proj, rpb, w_f32)


RET_NCHUNK = SEQ // RET_CHUNK


RET_NCAST = 4


def _ret_kernel(q_ref, k_ref, v_ref, g_ref, dmat_ref, dec_ref, cdec_ref, gn_ref, *refs, next_w_in):
    n_in = RET_NCAST + (2 if next_w_in else 0)
    o_ref, st_ref = refs[n_in], refs[-1]
    _cast_slabs(refs[:RET_NCAST], refs[n_in + 1:n_in + 1 + RET_NCAST])
    if next_w_in:
        win_ref, scale_ref, win_out_ref = refs[RET_NCAST], refs[RET_NCAST + 1], refs[n_in + 1 + RET_NCAST]
        win_out_ref[...] = (win_ref[...] * scale_ref[...]).astype(win_out_ref.dtype)

    def decayed(t, dec):
        return (t.astype(F32) * dec).astype(BF16)

    chunk = lambda c: slice(c * RET_CHUNK, (c + 1) * RET_CHUNK)
    last = RET_NCHUNK - 1

    for h in range(RET_HEADS):
        hd = slice(h * RET_HEAD_DIM, (h + 1) * RET_HEAD_DIM)
        dmat = dmat_ref[h]
        q_f, k_f, q_b, k_b = dec_ref[h, 0], dec_ref[h, 1], dec_ref[h, 2], dec_ref[h, 3]
        c_f, c_b = cdec_ref[h, 0:1, :], cdec_ref[h, 1:2, :]

        def summary(c, kdec):
            return lax.dot_general(decayed(k_ref[chunk(c), hd], kdec), v_ref[chunk(c), hd],
                                   (((0,), (0,)), ((), ())), preferred_element_type=F32)

        kv_f = [summary(c, k_f) for c in range(last)]
        kv_b = [None] + [summary(c, k_b) for c in range(1, RET_NCHUNK)]

        state = None
        for c in range(1, RET_NCHUNK):
            state = kv_f[c - 1] if state is None else state * c_f + kv_f[c - 1]
            st_ref[h, 0, c] = state.astype(BF16)
        state = None
        for c in range(last - 1, -1, -1):
            state = kv_b[c + 1] if state is None else state * c_b + kv_b[c + 1]
            st_ref[h, 1, c] = state.astype(BF16)

        gn = gn_ref[:, hd]
        for c in range(RET_NCHUNK):
            qc, kc, vc = q_ref[chunk(c), hd], k_ref[chunk(c), hd], v_ref[chunk(c), hd]
            s = lax.dot_general(qc, kc, (((1,), (1,)), ((), ())), preferred_element_type=F32) * dmat
            y = jnp.dot(s.astype(BF16), vc, preferred_element_type=F32)
            if c > 0:
                y = y + jnp.dot(decayed(qc, q_f), st_ref[h, 0, c], preferred_element_type=F32)
            if c < last:
                y = y + jnp.dot(decayed(qc, q_b), st_ref[h, 1, c], preferred_element_type=F32)
            y = y * _rms_scale(y) * gn
            o_ref[chunk(c), hd] = (g_ref[chunk(c), hd].astype(F32) * y).astype(o_ref.dtype)


def _ret_tables():
    pos = np.arange(SEQ, dtype=np.float64)
    theta = 1.0 / (RET_THETA_BASE ** np.linspace(0.0, 1.0, RET_HEAD_DIM // 2))
    ang = pos[:, None] * theta[None, :]
    sign = np.tile(np.array([-1.0, 1.0]), RET_HEAD_DIM // 2)
    cos = np.repeat(np.cos(ang), 2, axis=1)
    sin = np.repeat(np.sin(ang), 2, axis=1) * sign
    hidx = np.arange(RET_HEADS, dtype=np.float64)
    lg_f = np.log1p(-np.exp2(-5.0 - hidx))[:, None]
    lg_b = np.log1p(-np.exp2(-5.5 - hidx))[:, None]
    idx = np.arange(RET_CHUNK, dtype=np.float64)
    diff = idx[:, None] - idx[None, :]
    dmat = np.where(diff >= 0, np.exp(lg_f[:, :, None] * np.maximum(diff, 0.0)),
                    np.exp(lg_b[:, :, None] * np.maximum(-diff, 0.0)))
    dec = np.stack([np.exp(lg_f * (idx + 1.0)), np.exp(lg_f * (RET_CHUNK - 1.0 - idx)),
                    np.exp(lg_b * (RET_CHUNK - idx)), np.exp(lg_b * idx)], axis=1)
    dec = np.broadcast_to(dec[..., None], (RET_HEADS, 4, RET_CHUNK, LANES))
    cdec = np.stack([np.exp(lg_f * RET_CHUNK), np.exp(lg_b * RET_CHUNK)], axis=1)
    cdec = np.broadcast_to(cdec, (RET_HEADS, 2, LANES))
    return tuple(jnp.asarray(np.ascontiguousarray(t), F32) for t in (cos, sin, dmat, dec, cdec))


def _retention(proj, gn, tables, side_weights, w_in, in_scale, layer):
    _, _, dmat, dec, cdec = tables
    blk = P_RET // RET_WIDTH
    tok = lambda off: pl.BlockSpec((SEQ, RET_WIDTH), lambda b: (b, blk + off))
    cast_in, cast_out, cast_shape = _cast_specs(side_weights, layer)
    next_w_in = layer + 1 < DEPTH
    if next_w_in:
        win_in, win_out, win_shape = _cast_specs([w_in], layer + 1)
        cast_in = cast_in + win_in + [pl.BlockSpec((1, IN_COLS), lambda b: (0, 0))]
        cast_out, cast_shape = cast_out + win_out, cast_shape + win_shape
    return pl.pallas_call(
        functools.partial(_ret_kernel, next_w_in=next_w_in),
        grid=(BATCH,),
        in_specs=[
            tok(0), tok(1), tok(2), tok(3),
            pl.BlockSpec((RET_HEADS, RET_CHUNK, RET_CHUNK), lambda b: (0, 0, 0)),
            pl.BlockSpec((RET_HEADS, 4, RET_CHUNK, LANES), lambda b: (0, 0, 0, 0)),
            pl.BlockSpec((RET_HEADS, 2, LANES), lambda b: (0, 0, 0)),
            _layer_spec((1, RET_WIDTH), lambda b: (layer, 0, 0)),
        ] + cast_in,
        out_specs=[pl.BlockSpec((SEQ, RET_WIDTH), lambda b: (b, 0))] + cast_out,
        out_shape=[jax.ShapeDtypeStruct((TOKENS, RET_WIDTH), BF16)] + cast_shape,
        scratch_shapes=[pltpu.VMEM((RET_HEADS, 2, RET_NCHUNK, RET_HEAD_DIM, RET_HEAD_DIM), BF16)],
        compiler_params=_cparams(("parallel",)),
        name="retention",
    )(proj, proj, proj, proj, dmat, dec, cdec, gn, *side_weights, *((w_in, in_scale) if next_w_in else ()))


LRU_SEGS = SUBLANES
LRU_SEG = SEQ // LRU_SEGS
LRU_PAD = SUBLANES
LRU_SLABS = LRU_WIDTH // LANES


def _softplus(x):
    return jnp.maximum(x, 0.0) + jnp.log1p(jnp.exp(-jnp.abs(x)))


def _lru_kernel(xc_ref, gc_ref, cw_ref, cb_ref, w_ref, bias_ref, lam_ref, wsrc_ref, o_ref, wdst_ref,
                xpad_ref, af_ref, ab_ref, bf_ref, bb_ref):
    _cast_slabs([wsrc_ref], [wdst_ref])
    a_ref, b_ref = (af_ref, ab_ref), (bf_ref, bb_ref)
    zeros = jnp.zeros((LRU_PAD, LRU_WIDTH), F32)
    xpad_ref[0:LRU_PAD, :] = zeros
    xpad_ref[LRU_PAD + SEQ:, :] = zeros
    xpad_ref[LRU_PAD:LRU_PAD + SEQ, :] = xc_ref[...].astype(F32)

    cw = cw_ref[...]
    cb = cb_ref[...]
    half_k = (-0.5 * LRU_C * LOG2E) * _softplus(-lam_ref[...])
    row = lax.broadcasted_iota(jnp.int32, (LRU_SEG, LANES), 0)

    def gates(seg, first_dir):
        t0 = pl.multiple_of(seg * LRU_SEG, LRU_SEG)
        xw = xpad_ref[pl.ds(t0, LRU_SEG + 2 * LRU_PAD), :]
        taps = [xw[LRU_PAD - 2 + j:LRU_PAD - 2 + j + LRU_SEG, :] for j in range(4)]
        xf = taps[0] * cw[0:1] + taps[1] * cw[1:2] + taps[2] * cw[2:3] + taps[3] * cw[3:4] + cb
        for s in range(LRU_SLABS):
            cols = slice(s * LANES, (s + 1) * LANES)
            xs = xf[:, cols]
            xh = 0.5 * xs
            z = jnp.dot(xs.astype(BF16), w_ref[s], preferred_element_type=F32)
            for dr in range(2):
                zr = z[:, (2 * dr) * LANES:(2 * dr + 1) * LANES] + bias_ref[2 * dr:2 * dr + 1, cols]
                zi = z[:, (2 * dr + 1) * LANES:(2 * dr + 2) * LANES] + bias_ref[2 * dr + 1:2 * dr + 2, cols]
                hk = half_k[dr:dr + 1, cols]
                a = jnp.exp2(jnp.tanh(zr) * hk + hk)
                if first_dir == dr:
                    a = jnp.where(row == (0 if dr == 0 else LRU_SEG - 1), 0.0, a)
                om = 1.0 - a * a
                mult = om * lax.rsqrt(jnp.maximum(om, 1e-30))
                a_ref[dr][s, pl.ds(seg, LRU_SEG, stride=LRU_SEGS), :] = a
                b_ref[dr][s, pl.ds(seg, LRU_SEG, stride=LRU_SEGS), :] = mult * (jnp.tanh(zi) * xh + xh)

    gates(0, 0)
    lax.fori_loop(1, LRU_SEGS - 1, lambda seg, c: (gates(seg, None), c)[1], 0)
    gates(LRU_SEGS - 1, 1)

    def group(dr, j):
        return pl.ds(pl.multiple_of((j if dr == 0 else LRU_SEG - 1 - j) * LRU_SEGS, LRU_SEGS), LRU_SEGS)

    def probe(j, carry):
        new = []
        for dr in range(2):
            for s in range(LRU_SLABS):
                h, p = carry[dr * LRU_SLABS + s]
                a = a_ref[dr][s, group(dr, j), :]
                new.append((a * h + b_ref[dr][s, group(dr, j), :], a * p))
        return tuple(new)

    init = (jnp.zeros((LRU_SEGS, LANES), F32), jnp.ones((LRU_SEGS, LANES), F32))
    ends = lax.fori_loop(0, LRU_SEG, probe, (init,) * (2 * LRU_SLABS), unroll=2)

    def carry_in(dr, s):
        h_end, p_end = ends[dr * LRU_SLABS + s]
        state = jnp.zeros((1, LANES), F32)
        rows_in = [None] * LRU_SEGS
        for seg in (range(LRU_SEGS) if dr == 0 else range(LRU_SEGS - 1, -1, -1)):
            rows_in[seg] = state
            state = h_end[seg:seg + 1, :] + p_end[seg:seg + 1, :] * state
        return jnp.concatenate(rows_in, axis=0)

    def scan(j, hs):
        new = []
        for dr in range(2):
            for s in range(LRU_SLABS):
                h = a_ref[dr][s, group(dr, j), :] * hs[dr * LRU_SLABS + s] + b_ref[dr][s, group(dr, j), :]
                b_ref[dr][s, group(dr, j), :] = h
                new.append(h)
        return tuple(new)

    lax.fori_loop(0, LRU_SEG, scan, tuple(carry_in(dr, s) for dr in range(2) for s in range(LRU_SLABS)),
                  unroll=2)

    def finish(seg, carry):
        t0 = pl.multiple_of(seg * LRU_SEG, LRU_SEG)
        g = jax.nn.gelu(gc_ref[pl.ds(t0, LRU_SEG), :].astype(F32), approximate=True)
        for s in range(LRU_SLABS):
            cols = slice(s * LANES, (s + 1) * LANES)
            y = (b_ref[0][s, pl.ds(seg, LRU_SEG, stride=LRU_SEGS), :]
                 + b_ref[1][s, pl.ds(seg, LRU_SEG, stride=LRU_SEGS), :])
            o_ref[pl.ds(t0, LRU_SEG), cols] = (g[:, cols] * y).astype(o_ref.dtype)
        return carry

    lax.fori_loop(0, LRU_SEGS, finish, 0)


def _lru_weights(wa, wx):
    def slab(w):
        w = 0.5 * w.reshape(DEPTH, LRU_SLABS, 2, LRU_BLOCK_DIM, LRU_BLOCK_DIM)
        z = jnp.zeros_like(w[:, :, 0])
        top = jnp.concatenate([w[:, :, 0], z], axis=-1)
        bot = jnp.concatenate([z, w[:, :, 1]], axis=-1)
        return jnp.concatenate([top, bot], axis=-2)
    return jnp.concatenate([slab(wa[:, 0]), slab(wx[:, 0]), slab(wa[:, 1]), slab(wx[:, 1])],
                           axis=-1).astype(BF16)


def _rglru(proj, conv_w, conv_b, w4, bias4, lam, w_f32, layer):
    blk = P_LRU // LRU_WIDTH
    full = lambda *shape: _layer_spec(shape, lambda b: (layer,) + (0,) * len(shape))
    state = pltpu.VMEM((LRU_SLABS, SEQ, LANES), F32)
    cast_in, cast_out, cast_shape = _cast_specs([w_f32], layer)
    return pl.pallas_call(
        _lru_kernel,
        grid=(BATCH,),
        in_specs=[
            pl.BlockSpec((SEQ, LRU_WIDTH), lambda b: (b, blk)),
            pl.BlockSpec((SEQ, LRU_WIDTH), lambda b: (b, blk + 1)),
            full(4, LRU_WIDTH), full(1, LRU_WIDTH),
            full(LRU_SLABS, LANES, 4 * LANES), full(4, LRU_WIDTH), full(2, LRU_WIDTH),
        ] + cast_in,
        out_specs=[pl.BlockSpec((SEQ, LRU_WIDTH), lambda b: (b, 0))] + cast_out,
        out_shape=[jax.ShapeDtypeStruct((TOKENS, LRU_WIDTH), BF16)] + cast_shape,
        scratch_shapes=[pltpu.VMEM((SEQ + 2 * LRU_PAD, LRU_WIDTH), F32), state, state, state, state],
        compiler_params=_cparams(("parallel",)),
        name="rglru",
    )(proj, proj, conv_w, conv_b, w4, bias4, lam, w_f32)


TAIL_TM = 512
MLP_TF = 512
GATE_HALF = D_MODEL // 2


def _merged_branches(y_refs, gate_refs, bg_ref, wb_ref):
    branches = tuple(zip(y_refs, gate_refs))
    halves = []
    for half in range(D_MODEL // GATE_HALF):
        cols = slice(half * GATE_HALF, (half + 1) * GATE_HALF)
        merged = None
        for n, (y_ref, gate_ref) in enumerate(branches):
            branch = jnp.dot(y_ref[...], wb_ref[n * NA_WIDTH:(n + 1) * NA_WIDTH, cols],
                             preferred_element_type=F32)
            gate = jax.nn.sigmoid(gate_ref[:, cols].astype(F32) + bg_ref[n:n + 1, cols])
            merged = gate * branch if merged is None else merged + gate * branch
        halves.append(merged.astype(BF16))
    return jnp.concatenate(halves, axis=1)


def _tail_kernel(x_ref, yn_ref, yr_ref, yl_ref, g0_ref, g1_ref, g2_ref, bg_ref, wb_ref, wo_ref,
                 g_ref, wu_ref, wd_ref, p_ref, gp_ref, wg_ref, wp_ref, gf_ref, o_ref, *, final):
    merged = _merged_branches((yn_ref, yr_ref, yl_ref), (g0_ref, g1_ref, g2_ref), bg_ref, wb_ref)
    x = x_ref[...] + jnp.dot(merged, wo_ref[...], preferred_element_type=F32)
    h = (x * _rms_scale(x) * g_ref[...]).astype(BF16)
    o_ref[...] = x
    for c in range(D_FF // MLP_TF):
        ff = slice(c * MLP_TF, (c + 1) * MLP_TF)
        u = jnp.maximum(jnp.dot(h, wu_ref[:, ff], preferred_element_type=F32), 0.0)
        o_ref[...] += jnp.dot((u * u).astype(BF16), wd_ref[ff, :], preferred_element_type=F32)
    x = o_ref[...]
    h = (x * _rms_scale(x) * gp_ref[...]).astype(BF16)
    gate = jax.nn.sigmoid(jnp.dot(h, wg_ref[...], preferred_element_type=F32))
    emb = jnp.dot(p_ref[...].astype(BF16), wp_ref[...], preferred_element_type=F32)
    x = x + gate * emb
    if final:
        x = x * _rms_scale(x) * gf_ref[...]
    o_ref[...] = x


def _tail(x, y_na, y_ret, y_lru, proj, b_gate, w_branch, w_out, g_mlp, w_up, w_down, p, g_ple, w_gate, w_ple,
          g_final, layer):
    xtile = pl.BlockSpec((TAIL_TM, D_MODEL), lambda i: (i, 0))
    ytile = pl.BlockSpec((TAIL_TM, NA_WIDTH), lambda i: (i, 0))
    gate = lambda n: pl.BlockSpec((TAIL_TM, D_MODEL), lambda i: (i, P_GATE // D_MODEL + n))
    vec = _layer_spec((1, D_MODEL), lambda i: (layer, 0, 0))
    resident = lambda *shape: _resident_spec(shape)
    final = layer == DEPTH - 1
    return pl.pallas_call(
        functools.partial(_tail_kernel, final=final),
        grid=(TOKENS // TAIL_TM,),
        in_specs=[
            xtile, ytile, ytile, ytile, gate(0), gate(1), gate(2),
            _layer_spec((N_BRANCH, D_MODEL), lambda i: (layer, 0, 0)),
            resident(N_BRANCH * NA_WIDTH, D_MODEL),
            resident(D_MODEL, D_MODEL),
            vec,
            resident(D_MODEL, D_FF),
            resident(D_FF, D_MODEL),
            _layer_spec((TAIL_TM, PLE_DIM), lambda i: (layer, i, 0)),
            vec,
            resident(D_MODEL, D_MODEL),
            resident(PLE_DIM, D_MODEL),
            pl.BlockSpec((1, D_MODEL), lambda i: (0, 0)),
        ],
        out_specs=xtile,
        out_shape=jax.ShapeDtypeStruct((TOKENS, D_MODEL), F32),
        compiler_params=_cparams(("parallel",)),
        name="tail_final" if final else "tail",
    )(x, y_na, y_ret, y_lru, proj, proj, proj, b_gate, w_branch, w_out,
      g_mlp, w_up, w_down, p, g_ple, w_gate, w_ple, g_final)


_IN_COL_SCALE = np.ones((IN_COLS,), np.float32)
_IN_COL_SCALE[W_NA:W_NA + NA_WIDTH] = NA_QSCALE


def kernel(x, p, g_mix, w_in, b_gate, na_rpb, ret_gn, conv_w, conv_b, lru_wa, lru_ba, lru_wx, lru_bx,
           lru_lambda, w_branch, w_out, g_mlp, w_up, w_down, g_ple, w_ple_gate, w_ple, g_final):
    rows = lambda v: v.reshape(DEPTH, 1, -1).astype(F32)
    in_scale = jnp.asarray(_IN_COL_SCALE).reshape(1, IN_COLS)
    w_in_b = (w_in[0] * in_scale).astype(BF16)
    side_weights = (w_branch.reshape(DEPTH, N_BRANCH * NA_WIDTH, D_MODEL), w_out, w_ple_gate, w_ple)
    g_mix_r, g_mlp_r, g_ple_r, gn_r, conv_b_r = rows(g_mix), rows(g_mlp), rows(g_ple), rows(ret_gn), rows(conv_b)
    rpb = _na_pad_rpb(na_rpb)
    lru_w4 = _lru_weights(lru_wa, lru_wx)
    lru_bias4 = 0.5 * jnp.stack([lru_ba[:, 0], lru_bx[:, 0], lru_ba[:, 1], lru_bx[:, 1]], axis=1).astype(F32)
    ret_tables = _ret_tables()
    pt = p.reshape(DEPTH, TOKENS, PLE_DIM)
    g_final_r = g_final.reshape(1, D_MODEL).astype(F32)

    xt = x.reshape(TOKENS, D_MODEL)
    for i in range(DEPTH):
        proj = _in_proj(xt, g_mix_r, w_in_b, ret_tables[0], ret_tables[1], i)
        y_na, w_up_b = _na(proj, rpb, w_up, i)
        y_ret, w_branch_b, w_out_b, w_gate_b, w_ple_b, *next_w_in = _retention(
            proj, gn_r, ret_tables, side_weights, w_in, in_scale, i)
        y_lru, w_down_b = _rglru(proj, conv_w.astype(F32), conv_b_r, lru_w4, lru_bias4,
                                 lru_lambda.astype(F32), w_down, i)
        xt = _tail(xt, y_na, y_ret, y_lru, proj, b_gate.astype(F32), w_branch_b, w_out_b, g_mlp_r, w_up_b,
                   w_down_b, pt, g_ple_r, w_gate_b, w_ple_b, g_final_r, i)
        if next_w_in:
            w_in_b = next_w_in[0]
    return xt.reshape(BATCH, SEQ, D_MODEL)
```

```python
import functools
import math

import numpy as np
import jax
import jax.numpy as jnp
from jax import lax
from jax.experimental import pallas as pl
from jax.experimental.pallas import tpu as pltpu

F32 = jnp.float32
BF16 = jnp.bfloat16

D_MODEL = 1024
BATCH = 8
SEQ = 2048
DEPTH = 2
TOKENS = BATCH * SEQ

GRID_W = 64
GRID_H = SEQ // GRID_W
WIN_H = 8
WIN_W = 16
NA_HEADS = 8
NA_HEAD_DIM = 64
NA_WIDTH = NA_HEADS * NA_HEAD_DIM
RET_HEADS = 4
RET_HEAD_DIM = 128
RET_WIDTH = RET_HEADS * RET_HEAD_DIM
RET_THETA_BASE = 10000.0
LRU_WIDTH = 512
LRU_BLOCKS = 8
LRU_BLOCK_DIM = LRU_WIDTH // LRU_BLOCKS
LRU_C = 8.0
N_BRANCH = 3
D_FF = 4 * D_MODEL
PLE_DIM = 256
RMS_EPS = 1e-6

W_NA = 0
W_RET = 3 * NA_WIDTH
W_LRU = W_RET + 4 * RET_WIDTH
W_GATE = W_LRU + 2 * LRU_WIDTH
IN_COLS = W_GATE + N_BRANCH * D_MODEL
P_GATE = 0
P_NA = N_BRANCH * D_MODEL
P_RET = P_NA + W_RET - W_NA
P_LRU = P_NA + W_LRU - W_NA

LANES = 128
SUBLANES = 8
VMEM_LIMIT = 56 * 1024 * 1024

RET_CHUNK = 256
MASK_NEG = -1e30
LOG2E = math.log2(math.e)


def _cparams(sem):
    return pltpu.CompilerParams(dimension_semantics=sem, vmem_limit_bytes=VMEM_LIMIT)


def _rms_scale(x):
    return lax.rsqrt(jnp.mean(x * x, axis=-1, keepdims=True) + RMS_EPS)


def _layer_spec(shape, index_map):
    return pl.BlockSpec((None,) + shape, index_map)


def _resident_spec(shape):
    return pl.BlockSpec(shape, lambda i: (0,) * len(shape), pipeline_mode=pl.Buffered(1))


def _cast_specs(weights, layer):
    ins = [pl.BlockSpec((None, w.shape[1] // BATCH, w.shape[2]), lambda b: (layer, b, 0)) for w in weights]
    outs = [pl.BlockSpec((w.shape[1] // BATCH, w.shape[2]), lambda b: (b, 0)) for w in weights]
    shapes = [jax.ShapeDtypeStruct(w.shape[1:], BF16) for w in weights]
    return ins, outs, shapes


def _cast_slabs(src_refs, dst_refs):
    for src, dst in zip(src_refs, dst_refs):
        dst[...] = src[...].astype(dst.dtype)


IN_TM = 512
IN_TN = 512


def _inproj_kernel(x_ref, g_ref, w_ref, cos_ref, sin_ref, o_ref):
    x = x_ref[...]
    h = (x * _rms_scale(x) * g_ref[...]).astype(BF16)
    lane = lax.broadcasted_iota(jnp.int32, (IN_TM, LANES), 1)
    even = (lane & 1) == 0
    cos = cos_ref[...]
    sin = sin_ref[...]

    def rotate(t, scale):
        heads = []
        for hd in range(IN_TN // RET_HEAD_DIM):
            th = t[:, hd * RET_HEAD_DIM:(hd + 1) * RET_HEAD_DIM]
            partner = jnp.where(even, pltpu.roll(th, LANES - 1, 1), pltpu.roll(th, 1, 1))
            th = th * cos + partner * sin
            heads.append(th if scale is None else th * scale)
        return jnp.concatenate(heads, axis=1)

    ret_q, ret_k, ret_g = P_RET // IN_TN, P_RET // IN_TN + 1, P_RET // IN_TN + 3
    for c in range(IN_COLS // IN_TN):
        cols = slice(c * IN_TN, (c + 1) * IN_TN)
        src = c * IN_TN + (W_GATE - P_GATE if c * IN_TN < P_NA else W_NA - P_NA)
        r = jnp.dot(h, w_ref[:, src:src + IN_TN], preferred_element_type=F32)
        if c == ret_q:
            r = rotate(r, RET_HEAD_DIM ** -0.5)
        elif c == ret_k:
            r = rotate(r, None)
        elif c == ret_g:
            r = r * jax.nn.sigmoid(r)
        o_ref[:, cols] = r.astype(o_ref.dtype)


def _in_proj(x, g, w, cos, sin, layer):
    rot = pl.BlockSpec((IN_TM, RET_HEAD_DIM), lambda i: (i % (SEQ // IN_TM), 0))
    return pl.pallas_call(
        _inproj_kernel,
        grid=(TOKENS // IN_TM,),
        in_specs=[
            pl.BlockSpec((IN_TM, D_MODEL), lambda i: (i, 0)),
            _layer_spec((1, D_MODEL), lambda i: (layer, 0, 0)),
            _resident_spec((D_MODEL, IN_COLS)),
            rot, rot,
        ],
        out_specs=pl.BlockSpec((IN_TM, IN_COLS), lambda i: (i, 0)),
        out_shape=jax.ShapeDtypeStruct((TOKENS, IN_COLS), BF16),
        compiler_params=_cparams(("parallel",)),
        name="in_proj",
    )(x, g, w, cos, sin)


NA_PAIRS = NA_HEADS // 2
NA_KEYS = WIN_H * GRID_W
NA_BIAS_ROWS = 2 * WIN_H - 2
NA_QSCALE = NA_HEAD_DIM ** -0.5 * LOG2E
NA_ROWS_PER_STEP = 8


def _na_kernel(q_ref, k_ref, v_ref, rpb_ref, wsrc_ref, *refs, next_w_in):
    if next_w_in:
        win_ref, scale_ref, o_ref, wdst_ref, win_out_ref, s_ref, t2_ref = refs
        win_out_ref[...] = (win_ref[...] * scale_ref[...]).astype(win_out_ref.dtype)
    else:
        o_ref, wdst_ref, s_ref, t2_ref = refs
    _cast_slabs([wsrc_ref], [wdst_ref])
    lane = lax.broadcasted_iota(jnp.int32, (GRID_W, LANES), 1)
    lo = lane < NA_HEAD_DIM

    @pl.when(pl.program_id(0) == 0)
    def _():
        qcol = lax.broadcasted_iota(jnp.int32, (GRID_W, LANES), 0)
        kcol = jnp.where(lo, lane, lane - GRID_W)
        col_start = jnp.clip(qcol - WIN_W // 2, 0, GRID_W - WIN_W)
        off = kcol - col_start

        def build(ridx, carry):
            for h in range(NA_HEADS):
                xa = jnp.broadcast_to(rpb_ref[h, pl.ds(ridx, 1), :], (GRID_W, LANES))
                xb = jnp.broadcast_to(rpb_ref[h, pl.ds(ridx + 1, 1), :], (GRID_W, LANES))
                ta = pltpu.roll(xa, GRID_W + 1, 1, stride=1, stride_axis=0)
                tb = pltpu.roll(xb, 1, 1, stride=1, stride_axis=0)
                tile = jnp.where(off >= 0, jnp.where(off < WIN_W, jnp.where(lo, ta, tb), MASK_NEG), MASK_NEG)
                t2_ref[h // 2, ridx, (h % 2) * GRID_W:(h % 2 + 1) * GRID_W, :] = tile
            return carry

        lax.fori_loop(0, NA_BIAS_ROWS, build, 0)

    def window(r):
        if isinstance(r, int):
            r_start = min(max(r - WIN_H // 2, 0), GRID_H - WIN_H)
            return r_start * GRID_W, r_start - r + (WIN_H - 1)
        r_start = jnp.clip(r - WIN_H // 2, 0, GRID_H - WIN_H)
        return pl.multiple_of(r_start * GRID_W, GRID_W), r_start - r + (WIN_H - 1)

    def token_row(r):
        return r * GRID_W if isinstance(r, int) else pl.multiple_of(r * GRID_W, GRID_W)

    def scores(r, slot):
        kw0, d = window(r)
        q_r = q_ref[pl.ds(token_row(r), GRID_W), :]
        for p in range(NA_PAIRS):
            cols = slice(p * LANES, (p + 1) * LANES)
            qp = q_r[:, cols]
            zero = jnp.zeros_like(qp)
            q2 = jnp.concatenate([jnp.where(lo, qp, zero), jnp.where(lo, zero, qp)], axis=0)
            kp = k_ref[pl.ds(kw0, NA_KEYS), cols]
            s = lax.dot_general(q2, kp, (((1,), (1,)), ((), ())), preferred_element_type=F32)
            for c in range(NA_KEYS // LANES):
                kc = slice(c * LANES, (c + 1) * LANES)
                s_ref[slot, p, :, kc] = s[:, kc] + t2_ref[p, d + 2 * c]

    def attend(r, slot):
        kw0, _ = window(r)
        outs = []
        for p in range(NA_PAIRS):
            cols = slice(p * LANES, (p + 1) * LANES)
            s = s_ref[slot, p]
            e = jnp.exp2(s - jnp.max(s, axis=-1, keepdims=True))
            l = jnp.sum(e, axis=-1, keepdims=True)
            vp = v_ref[pl.ds(kw0, NA_KEYS), cols]
            o = jnp.dot(e.astype(BF16), vp, preferred_element_type=F32) / l
            outs.append(jnp.where(lo, o[:GRID_W], o[GRID_W:]))
        o_ref[pl.ds(token_row(r), GRID_W), :] = (
            jnp.concatenate(outs, axis=1).astype(o_ref.dtype))

    scores(0, 0)

    def rows(j, carry):
        r0 = NA_ROWS_PER_STEP * j
        for i in range(NA_ROWS_PER_STEP):
            scores(jnp.minimum(r0 + i + 1, GRID_H - 1), (i + 1) % 2)
            attend(r0 + i, i % 2)
        return carry

    lax.fori_loop(0, GRID_H // NA_ROWS_PER_STEP, rows, 0)


NA_RPB_ROWS = 2 * WIN_H
NA_RPB_LANE0 = GRID_W - WIN_W


def _na_pad_rpb(rpb):
    pad = ((0, 0), (0, 0), (0, NA_RPB_ROWS - rpb.shape[2]),
           (NA_RPB_LANE0, LANES - rpb.shape[3] - NA_RPB_LANE0))
    return jnp.pad(rpb.astype(F32) * LOG2E, pad)


def _na(proj, rpb, w_f32, w_in, in_scale, layer):
    blk = P_NA // NA_WIDTH
    cast_in, cast_out, cast_shape = _cast_specs([w_f32], layer)
    next_w_in = layer + 1 < DEPTH
    if next_w_in:
        win_in, win_out, win_shape = _cast_specs([w_in], layer + 1)
        cast_in = cast_in + win_in + [pl.BlockSpec((1, IN_COLS), lambda b: (0, 0))]
        cast_out, cast_shape = cast_out + win_out, cast_shape + win_shape
    return pl.pallas_call(
        functools.partial(_na_kernel, next_w_in=next_w_in),
        grid=(BATCH,),
        in_specs=[
            pl.BlockSpec((SEQ, NA_WIDTH), lambda b: (b, blk)),
            pl.BlockSpec((SEQ, NA_WIDTH), lambda b: (b, blk + 1)),
            pl.BlockSpec((SEQ, NA_WIDTH), lambda b: (b, blk + 2)),
            _layer_spec((NA_HEADS, NA_RPB_ROWS, LANES), lambda b: (layer, 0, 0, 0)),
        ] + cast_in,
        out_specs=[pl.BlockSpec((SEQ, NA_WIDTH), lambda b: (b, 0))] + cast_out,
        out_shape=[jax.ShapeDtypeStruct((TOKENS, NA_WIDTH), BF16)] + cast_shape,
        scratch_shapes=[pltpu.VMEM((2, NA_PAIRS, 2 * GRID_W, NA_KEYS), F32),
                        pltpu.VMEM((NA_PAIRS, NA_BIAS_ROWS, 2 * GRID_W, LANES), F32)],
        compiler_params=_cparams(("arbitrary",)),
        name="na",
    )(proj, proj, proj, rpb, w_f32, *((w_in, in_scale) if next_w_in else ()))


RET_NCHUNK = SEQ // RET_CHUNK


def _ret_kernel(q_ref, k_ref, v_ref, g_ref, dmat_ref, dec_ref, cdec_ref, gn_ref, o_ref, st_ref):
    def decayed(t, dec):
        return (t.astype(F32) * dec).astype(BF16)

    chunk = lambda c: slice(c * RET_CHUNK, (c + 1) * RET_CHUNK)
    last = RET_NCHUNK - 1

    for h in range(RET_HEADS):
        hd = slice(h * RET_HEAD_DIM, (h + 1) * RET_HEAD_DIM)
        dmat = dmat_ref[h]
        q_f, k_f, q_b, k_b = dec_ref[h, 0], dec_ref[h, 1], dec_ref[h, 2], dec_ref[h, 3]
        c_f, c_b = cdec_ref[h, 0:1, :], cdec_ref[h, 1:2, :]

        def summary(c, kdec):
            return lax.dot_general(decayed(k_ref[chunk(c), hd], kdec), v_ref[chunk(c), hd],
                                   (((0,), (0,)), ((), ())), preferred_element_type=F32)

        kv_f = [summary(c, k_f) for c in range(last)]
        kv_b = [None] + [summary(c, k_b) for c in range(1, RET_NCHUNK)]

        state = None
        for c in range(1, RET_NCHUNK):
            state = kv_f[c - 1] if state is None else state * c_f + kv_f[c - 1]
            st_ref[h, 0, c] = state.astype(BF16)
        state = None
        for c in range(last - 1, -1, -1):
            state = kv_b[c + 1] if state is None else state * c_b + kv_b[c + 1]
            st_ref[h, 1, c] = state.astype(BF16)

        gn = gn_ref[:, hd]
        for c in range(RET_NCHUNK):
            qc, kc, vc = q_ref[chunk(c), hd], k_ref[chunk(c), hd], v_ref[chunk(c), hd]
            s = lax.dot_general(qc, kc, (((1,), (1,)), ((), ())), preferred_element_type=F32) * dmat
            y = jnp.dot(s.astype(BF16), vc, preferred_element_type=F32)
            if c > 0:
                y = y + jnp.dot(decayed(qc, q_f), st_ref[h, 0, c], preferred_element_type=F32)
            if c < last:
                y = y + jnp.dot(decayed(qc, q_b), st_ref[h, 1, c], preferred_element_type=F32)
            y = y * _rms_scale(y) * gn
            o_ref[chunk(c), hd] = (g_ref[chunk(c), hd].astype(F32) * y).astype(o_ref.dtype)


def _ret_tables():
    pos = np.arange(SEQ, dtype=np.float64)
    theta = 1.0 / (RET_THETA_BASE ** np.linspace(0.0, 1.0, RET_HEAD_DIM // 2))
    ang = pos[:, None] * theta[None, :]
    sign = np.tile(np.array([-1.0, 1.0]), RET_HEAD_DIM // 2)
    cos = np.repeat(np.cos(ang), 2, axis=1)
    sin = np.repeat(np.sin(ang), 2, axis=1) * sign
    hidx = np.arange(RET_HEADS, dtype=np.float64)
    lg_f = np.log1p(-np.exp2(-5.0 - hidx))[:, None]
    lg_b = np.log1p(-np.exp2(-5.5 - hidx))[:, None]
    idx = np.arange(RET_CHUNK, dtype=np.float64)
    diff = idx[:, None] - idx[None, :]
    dmat = np.where(diff >= 0, np.exp(lg_f[:, :, None] * np.maximum(diff, 0.0)),
                    np.exp(lg_b[:, :, None] * np.maximum(-diff, 0.0)))
    dec = np.stack([np.exp(lg_f * (idx + 1.0)), np.exp(lg_f * (RET_CHUNK - 1.0 - idx)),
                    np.exp(lg_b * (RET_CHUNK - idx)), np.exp(lg_b * idx)], axis=1)
    dec = np.broadcast_to(dec[..., None], (RET_HEADS, 4, RET_CHUNK, LANES))
    cdec = np.stack([np.exp(lg_f * RET_CHUNK), np.exp(lg_b * RET_CHUNK)], axis=1)
    cdec = np.broadcast_to(cdec, (RET_HEADS, 2, LANES))
    return tuple(jnp.asarray(np.ascontiguousarray(t), F32) for t in (cos, sin, dmat, dec, cdec))


def _retention(proj, gn, tables, layer):
    _, _, dmat, dec, cdec = tables
    blk = P_RET // RET_WIDTH
    tok = lambda off: pl.BlockSpec((SEQ, RET_WIDTH), lambda b: (b, blk + off))
    return pl.pallas_call(
        _ret_kernel,
        grid=(BATCH,),
        in_specs=[
            tok(0), tok(1), tok(2), tok(3),
            pl.BlockSpec((RET_HEADS, RET_CHUNK, RET_CHUNK), lambda b: (0, 0, 0)),
            pl.BlockSpec((RET_HEADS, 4, RET_CHUNK, LANES), lambda b: (0, 0, 0, 0)),
            pl.BlockSpec((RET_HEADS, 2, LANES), lambda b: (0, 0, 0)),
            _layer_spec((1, RET_WIDTH), lambda b: (layer, 0, 0)),
        ],
        out_specs=pl.BlockSpec((SEQ, RET_WIDTH), lambda b: (b, 0)),
        out_shape=jax.ShapeDtypeStruct((TOKENS, RET_WIDTH), BF16),
        scratch_shapes=[pltpu.VMEM((RET_HEADS, 2, RET_NCHUNK, RET_HEAD_DIM, RET_HEAD_DIM), BF16)],
        compiler_params=_cparams(("parallel",)),
        name="retention",
    )(proj, proj, proj, proj, dmat, dec, cdec, gn)


LRU_SEGS = SUBLANES
LRU_SEG = SEQ // LRU_SEGS
LRU_PAD = SUBLANES
LRU_SLABS = LRU_WIDTH // LANES


def _softplus(x):
    return jnp.maximum(x, 0.0) + jnp.log1p(jnp.exp(-jnp.abs(x)))


LRU_NCAST = 5


def _lru_kernel(xc_ref, gc_ref, cw_ref, cb_ref, w_ref, bias_ref, lam_ref, *refs):
    o_ref = refs[LRU_NCAST]
    xpad_ref, af_ref, ab_ref, bf_ref, bb_ref = refs[2 * LRU_NCAST + 1:]
    _cast_slabs(refs[:LRU_NCAST], refs[LRU_NCAST + 1:2 * LRU_NCAST + 1])
    a_ref, b_ref = (af_ref, ab_ref), (bf_ref, bb_ref)
    zeros = jnp.zeros((LRU_PAD, LRU_WIDTH), F32)
    xpad_ref[0:LRU_PAD, :] = zeros
    xpad_ref[LRU_PAD + SEQ:, :] = zeros
    xpad_ref[LRU_PAD:LRU_PAD + SEQ, :] = xc_ref[...].astype(F32)

    cw = cw_ref[...]
    cb = cb_ref[...]
    half_k = (-0.5 * LRU_C * LOG2E) * _softplus(-lam_ref[...])
    row = lax.broadcasted_iota(jnp.int32, (LRU_SEG, LANES), 0)

    def gates(seg, first_dir):
        t0 = pl.multiple_of(seg * LRU_SEG, LRU_SEG)
        xw = xpad_ref[pl.ds(t0, LRU_SEG + 2 * LRU_PAD), :]
        taps = [xw[LRU_PAD - 2 + j:LRU_PAD - 2 + j + LRU_SEG, :] for j in range(4)]
        xf = taps[0] * cw[0:1] + taps[1] * cw[1:2] + taps[2] * cw[2:3] + taps[3] * cw[3:4] + cb
        for s in range(LRU_SLABS):
            cols = slice(s * LANES, (s + 1) * LANES)
            xs = xf[:, cols]
            xh = 0.5 * xs
            z = jnp.dot(xs.astype(BF16), w_ref[s], preferred_element_type=F32)
            for dr in range(2):
                zr = z[:, (2 * dr) * LANES:(2 * dr + 1) * LANES] + bias_ref[2 * dr:2 * dr + 1, cols]
                zi = z[:, (2 * dr + 1) * LANES:(2 * dr + 2) * LANES] + bias_ref[2 * dr + 1:2 * dr + 2, cols]
                hk = half_k[dr:dr + 1, cols]
                a = jnp.exp2(jnp.tanh(zr) * hk + hk)
                if first_dir == dr:
                    a = jnp.where(row == (0 if dr == 0 else LRU_SEG - 1), 0.0, a)
                om = 1.0 - a * a
                mult = om * lax.rsqrt(jnp.maximum(om, 1e-30))
                a_ref[dr][s, pl.ds(seg, LRU_SEG, stride=LRU_SEGS), :] = a
                b_ref[dr][s, pl.ds(seg, LRU_SEG, stride=LRU_SEGS), :] = mult * (jnp.tanh(zi) * xh + xh)

    gates(0, 0)
    lax.fori_loop(1, LRU_SEGS - 1, lambda seg, c: (gates(seg, None), c)[1], 0)
    gates(LRU_SEGS - 1, 1)

    def group(dr, j):
        return pl.ds(pl.multiple_of((j if dr == 0 else LRU_SEG - 1 - j) * LRU_SEGS, LRU_SEGS), LRU_SEGS)

    def probe(j, carry):
        new = []
        for dr in range(2):
            for s in range(LRU_SLABS):
                h, p = carry[dr * LRU_SLABS + s]
                a = a_ref[dr][s, group(dr, j), :]
                new.append((a * h + b_ref[dr][s, group(dr, j), :], a * p))
        return tuple(new)

    init = (jnp.zeros((LRU_SEGS, LANES), F32), jnp.ones((LRU_SEGS, LANES), F32))
    ends = lax.fori_loop(0, LRU_SEG, probe, (init,) * (2 * LRU_SLABS), unroll=2)

    def carry_in(dr, s):
        h_end, p_end = ends[dr * LRU_SLABS + s]
        state = jnp.zeros((1, LANES), F32)
        rows_in = [None] * LRU_SEGS
        for seg in (range(LRU_SEGS) if dr == 0 else range(LRU_SEGS - 1, -1, -1)):
            rows_in[seg] = state
            state = h_end[seg:seg + 1, :] + p_end[seg:seg + 1, :] * state
        return jnp.concatenate(rows_in, axis=0)

    def scan(j, hs):
        new = []
        for dr in range(2):
            for s in range(LRU_SLABS):
                h = a_ref[dr][s, group(dr, j), :] * hs[dr * LRU_SLABS + s] + b_ref[dr][s, group(dr, j), :]
                b_ref[dr][s, group(dr, j), :] = h
                new.append(h)
        return tuple(new)

    lax.fori_loop(0, LRU_SEG, scan, tuple(carry_in(dr, s) for dr in range(2) for s in range(LRU_SLABS)),
                  unroll=2)

    def finish(seg, carry):
        t0 = pl.multiple_of(seg * LRU_SEG, LRU_SEG)
        g = jax.nn.gelu(gc_ref[pl.ds(t0, LRU_SEG), :].astype(F32), approximate=True)
        for s in range(LRU_SLABS):
            cols = slice(s * LANES, (s + 1) * LANES)
            y = (b_ref[0][s, pl.ds(seg, LRU_SEG, stride=LRU_SEGS), :]
                 + b_ref[1][s, pl.ds(seg, LRU_SEG, stride=LRU_SEGS), :])
            o_ref[pl.ds(t0, LRU_SEG), cols] = (g[:, cols] * y).astype(o_ref.dtype)
        return carry

    lax.fori_loop(0, LRU_SEGS, finish, 0)


def _lru_weights(wa, wx):
    def slab(w):
        w = 0.5 * w.reshape(DEPTH, LRU_SLABS, 2, LRU_BLOCK_DIM, LRU_BLOCK_DIM)
        z = jnp.zeros_like(w[:, :, 0])
        top = jnp.concatenate([w[:, :, 0], z], axis=-1)
        bot = jnp.concatenate([z, w[:, :, 1]], axis=-1)
        return jnp.concatenate([top, bot], axis=-2)
    return jnp.concatenate([slab(wa[:, 0]), slab(wx[:, 0]), slab(wa[:, 1]), slab(wx[:, 1])],
                           axis=-1).astype(BF16)


def _rglru(proj, conv_w, conv_b, w4, bias4, lam, side_weights, layer):
    blk = P_LRU // LRU_WIDTH
    full = lambda *shape: _layer_spec(shape, lambda b: (layer,) + (0,) * len(shape))
    state = pltpu.VMEM((LRU_SLABS, SEQ, LANES), F32)
    cast_in, cast_out, cast_shape = _cast_specs(side_weights, layer)
    return pl.pallas_call(
        _lru_kernel,
        grid=(BATCH,),
        in_specs=[
            pl.BlockSpec((SEQ, LRU_WIDTH), lambda b: (b, blk)),
            pl.BlockSpec((SEQ, LRU_WIDTH), lambda b: (b, blk + 1)),
            full(4, LRU_WIDTH), full(1, LRU_WIDTH),
            full(LRU_SLABS, LANES, 4 * LANES), full(4, LRU_WIDTH), full(2, LRU_WIDTH),
        ] + cast_in,
        out_specs=[pl.BlockSpec((SEQ, LRU_WIDTH), lambda b: (b, 0))] + cast_out,
        out_shape=[jax.ShapeDtypeStruct((TOKENS, LRU_WIDTH), BF16)] + cast_shape,
        scratch_shapes=[pltpu.VMEM((SEQ + 2 * LRU_PAD, LRU_WIDTH), F32), state, state, state, state],
        compiler_params=_cparams(("parallel",)),
        name="rglru",
    )(proj, proj, conv_w, conv_b, w4, bias4, lam, *side_weights)


TAIL_TM = 512
MLP_TF = 512
GATE_HALF = D_MODEL // 2


def _merged_branches(y_refs, gate_refs, bg_ref, wb_ref):
    branches = tuple(zip(y_refs, gate_refs))
    halves = []
    for half in range(D_MODEL // GATE_HALF):
        cols = slice(half * GATE_HALF, (half + 1) * GATE_HALF)
        merged = None
        for n, (y_ref, gate_ref) in enumerate(branches):
            branch = jnp.dot(y_ref[...], wb_ref[n * NA_WIDTH:(n + 1) * NA_WIDTH, cols],
                             preferred_element_type=F32)
            gate = jax.nn.sigmoid(gate_ref[:, cols].astype(F32) + bg_ref[n:n + 1, cols])
            merged = gate * branch if merged is None else merged + gate * branch
        halves.append(merged.astype(BF16))
    return jnp.concatenate(halves, axis=1)


def _tail_kernel(x_ref, yn_ref, yr_ref, yl_ref, g0_ref, g1_ref, g2_ref, bg_ref, wb_ref, wo_ref,
                 g_ref, wu_ref, wd_ref, p_ref, gp_ref, wg_ref, wp_ref, gf_ref, o_ref, *, final):
    merged = _merged_branches((yn_ref, yr_ref, yl_ref), (g0_ref, g1_ref, g2_ref), bg_ref, wb_ref)
    x = x_ref[...] + jnp.dot(merged, wo_ref[...], preferred_element_type=F32)
    h = (x * _rms_scale(x) * g_ref[...]).astype(BF16)
    o_ref[...] = x
    for c in range(D_FF // MLP_TF):
        ff = slice(c * MLP_TF, (c + 1) * MLP_TF)
        u = jnp.maximum(jnp.dot(h, wu_ref[:, ff], preferred_element_type=F32), 0.0)
        o_ref[...] += jnp.dot((u * u).astype(BF16), wd_ref[ff, :], preferred_element_type=F32)
    x = o_ref[...]
    h = (x * _rms_scale(x) * gp_ref[...]).astype(BF16)
    gate = jax.nn.sigmoid(jnp.dot(h, wg_ref[...], preferred_element_type=F32))
    emb = jnp.dot(p_ref[...].astype(BF16), wp_ref[...], preferred_element_type=F32)
    x = x + gate * emb
    if final:
        x = x * _rms_scale(x) * gf_ref[...]
    o_ref[...] = x


def _tail(x, y_na, y_ret, y_lru, proj, b_gate, w_branch, w_out, g_mlp, w_up, w_down, p, g_ple, w_gate, w_ple,
          g_final, layer):
    xtile = pl.BlockSpec((TAIL_TM, D_MODEL), lambda i: (i, 0))
    ytile = pl.BlockSpec((TAIL_TM, NA_WIDTH), lambda i: (i, 0))
    gate = lambda n: pl.BlockSpec((TAIL_TM, D_MODEL), lambda i: (i, P_GATE // D_MODEL + n))
    vec = _layer_spec((1, D_MODEL), lambda i: (layer, 0, 0))
    resident = lambda *shape: _resident_spec(shape)
    final = layer == DEPTH - 1
    return pl.pallas_call(
        functools.partial(_tail_kernel, final=final),
        grid=(TOKENS // TAIL_TM,),
        in_specs=[
            xtile, ytile, ytile, ytile, gate(0), gate(1), gate(2),
            _layer_spec((N_BRANCH, D_MODEL), lambda i: (layer, 0, 0)),
            resident(N_BRANCH * NA_WIDTH, D_MODEL),
            resident(D_MODEL, D_MODEL),
            vec,
            resident(D_MODEL, D_FF),
            resident(D_FF, D_MODEL),
            _layer_spec((TAIL_TM, PLE_DIM), lambda i: (layer, i, 0)),
            vec,
            resident(D_MODEL, D_MODEL),
            resident(PLE_DIM, D_MODEL),
            pl.BlockSpec((1, D_MODEL), lambda i: (0, 0)),
        ],
        out_specs=xtile,
        out_shape=jax.ShapeDtypeStruct((TOKENS, D_MODEL), F32),
        compiler_params=_cparams(("parallel",)),
        name="tail_final" if final else "tail",
    )(x, y_na, y_ret, y_lru, proj, proj, proj, b_gate, w_branch, w_out,
      g_mlp, w_up, w_down, p, g_ple, w_gate, w_ple, g_final)


_IN_COL_SCALE = np.ones((IN_COLS,), np.float32)
_IN_COL_SCALE[W_NA:W_NA + NA_WIDTH] = NA_QSCALE


def kernel(x, p, g_mix, w_in, b_gate, na_rpb, ret_gn, conv_w, conv_b, lru_wa, lru_ba, lru_wx, lru_bx,
           lru_lambda, w_branch, w_out, g_mlp, w_up, w_down, g_ple, w_ple_gate, w_ple, g_final):
    rows = lambda v: v.reshape(DEPTH, 1, -1).astype(F32)
    in_scale = jnp.asarray(_IN_COL_SCALE).reshape(1, IN_COLS)
    w_in_b = (w_in[0] * in_scale).astype(BF16)
    side_weights = (w_down, w_branch.reshape(DEPTH, N_BRANCH * NA_WIDTH, D_MODEL), w_out, w_ple_gate, w_ple)
    g_mix_r, g_mlp_r, g_ple_r, gn_r, conv_b_r = rows(g_mix), rows(g_mlp), rows(g_ple), rows(ret_gn), rows(conv_b)
    rpb = _na_pad_rpb(na_rpb)
    lru_w4 = _lru_weights(lru_wa, lru_wx)
    lru_bias4 = 0.5 * jnp.stack([lru_ba[:, 0], lru_bx[:, 0], lru_ba[:, 1], lru_bx[:, 1]], axis=1).astype(F32)
    ret_tables = _ret_tables()
    pt = p.reshape(DEPTH, TOKENS, PLE_DIM)
    g_final_r = g_final.reshape(1, D_MODEL).astype(F32)

    xt = x.reshape(TOKENS, D_MODEL)
    for i in range(DEPTH):
        proj = _in_proj(xt, g_mix_r, w_in_b, ret_tables[0], ret_tables[1], i)
        y_na, w_up_b, *next_w_in = _na(proj, rpb, w_up, w_in, in_scale, i)
        y_ret = _retention(proj, gn_r, ret_tables, i)
        y_lru, w_down_b, w_branch_b, w_out_b, w_gate_b, w_ple_b = _rglru(
            proj, conv_w.astype(F32), conv_b_r, lru_w4, lru_bias4, lru_lambda.astype(F32), side_weights, i)
        xt = _tail(xt, y_na, y_ret, y_lru, proj, b_gate.astype(F32), w_branch_b, w_out_b, g_mlp_r, w_up_b,
                   w_down_b, pt, g_ple_r, w_gate_b, w_ple_b, g_final_r, i)
        if next_w_in:
            w_in_b = next_w_in[0]
    return xt.reshape(BATCH, SEQ, D_MODEL)
```

```python
import functools
import math

import numpy as np
import jax
import jax.numpy as jnp
from jax import lax
from jax.experimental import pallas as pl
from jax.experimental.pallas import tpu as pltpu

F32 = jnp.float32
BF16 = jnp.bfloat16

D_MODEL = 1024
BATCH = 8
SEQ = 2048
DEPTH = 2
TOKENS = BATCH * SEQ

GRID_W = 64
GRID_H = SEQ // GRID_W
WIN_H = 8
WIN_W = 16
NA_HEADS = 8
NA_HEAD_DIM = 64
NA_WIDTH = NA_HEADS * NA_HEAD_DIM
RET_HEADS = 4
RET_HEAD_DIM = 128
RET_WIDTH = RET_HEADS * RET_HEAD_DIM
RET_THETA_BASE = 10000.0
LRU_WIDTH = 512
LRU_BLOCKS = 8
LRU_BLOCK_DIM = LRU_WIDTH // LRU_BLOCKS
LRU_C = 8.0
N_BRANCH = 3
D_FF = 4 * D_MODEL
PLE_DIM = 256
RMS_EPS = 1e-6

W_NA = 0
W_RET = 3 * NA_WIDTH
W_LRU = W_RET + 4 * RET_WIDTH
W_GATE = W_LRU + 2 * LRU_WIDTH
IN_COLS = W_GATE + N_BRANCH * D_MODEL
P_GATE = 0
P_NA = N_BRANCH * D_MODEL
P_RET = P_NA + W_RET - W_NA
P_LRU = P_NA + W_LRU - W_NA

LANES = 128
SUBLANES = 8
VMEM_LIMIT = 56 * 1024 * 1024

RET_CHUNK = 256
MASK_NEG = -1e30
LOG2E = math.log2(math.e)


def _cparams(sem):
    return pltpu.CompilerParams(dimension_semantics=sem, vmem_limit_bytes=VMEM_LIMIT)


def _rms_scale(x):
    return lax.rsqrt(jnp.mean(x * x, axis=-1, keepdims=True) + RMS_EPS)


def _layer_spec(shape, index_map):
    return pl.BlockSpec((None,) + shape, index_map)


def _resident_spec(shape):
    return pl.BlockSpec(shape, lambda i: (0,) * len(shape), pipeline_mode=pl.Buffered(1))


def _cast_specs(weights, layer):
    ins = [pl.BlockSpec((None, w.shape[1] // BATCH, w.shape[2]), lambda b: (layer, b, 0)) for w in weights]
    outs = [pl.BlockSpec((w.shape[1] // BATCH, w.shape[2]), lambda b: (b, 0)) for w in weights]
    shapes = [jax.ShapeDtypeStruct(w.shape[1:], BF16) for w in weights]
    return ins, outs, shapes


def _cast_slabs(src_refs, dst_refs):
    for src, dst in zip(src_refs, dst_refs):
        dst[...] = src[...].astype(dst.dtype)


IN_TM = 512
IN_TN = 512


def _inproj_kernel(x_ref, g_ref, w_ref, cos_ref, sin_ref, o_ref):
    x = x_ref[...]
    h = (x * _rms_scale(x) * g_ref[...]).astype(BF16)
    lane = lax.broadcasted_iota(jnp.int32, (IN_TM, LANES), 1)
    even = (lane & 1) == 0
    cos = cos_ref[...]
    sin = sin_ref[...]

    def rotate(t, scale):
        heads = []
        for hd in range(IN_TN // RET_HEAD_DIM):
            th = t[:, hd * RET_HEAD_DIM:(hd + 1) * RET_HEAD_DIM]
            partner = jnp.where(even, pltpu.roll(th, LANES - 1, 1), pltpu.roll(th, 1, 1))
            th = th * cos + partner * sin
            heads.append(th if scale is None else th * scale)
        return jnp.concatenate(heads, axis=1)

    ret_q, ret_k, ret_g = P_RET // IN_TN, P_RET // IN_TN + 1, P_RET // IN_TN + 3
    for c in range(IN_COLS // IN_TN):
        cols = slice(c * IN_TN, (c + 1) * IN_TN)
        src = c * IN_TN + (W_GATE - P_GATE if c * IN_TN < P_NA else W_NA - P_NA)
        r = jnp.dot(h, w_ref[:, src:src + IN_TN], preferred_element_type=F32)
        if c == ret_q:
            r = rotate(r, RET_HEAD_DIM ** -0.5)
        elif c == ret_k:
            r = rotate(r, None)
        elif c == ret_g:
            r = r * jax.nn.sigmoid(r)
        o_ref[:, cols] = r.astype(o_ref.dtype)


def _in_proj(x, g, w, cos, sin, layer):
    rot = pl.BlockSpec((IN_TM, RET_HEAD_DIM), lambda i: (i % (SEQ // IN_TM), 0))
    return pl.pallas_call(
        _inproj_kernel,
        grid=(TOKENS // IN_TM,),
        in_specs=[
            pl.BlockSpec((IN_TM, D_MODEL), lambda i: (i, 0)),
            _layer_spec((1, D_MODEL), lambda i: (layer, 0, 0)),
            _resident_spec((D_MODEL, IN_COLS)),
            rot, rot,
        ],
        out_specs=pl.BlockSpec((IN_TM, IN_COLS), lambda i: (i, 0)),
        out_shape=jax.ShapeDtypeStruct((TOKENS, IN_COLS), BF16),
        compiler_params=_cparams(("parallel",)),
        name="in_proj",
    )(x, g, w, cos, sin)


NA_PAIRS = NA_HEADS // 2
NA_KEYS = WIN_H * GRID_W
NA_BIAS_ROWS = 2 * WIN_H - 2
NA_QSCALE = NA_HEAD_DIM ** -0.5 * LOG2E
NA_ROWS_PER_STEP = 8


def _na_kernel(q_ref, k_ref, v_ref, rpb_ref, wsrc_ref, *refs, next_w_in):
    if next_w_in:
        win_ref, scale_ref, o_ref, wdst_ref, win_out_ref, s_ref, t2_ref = refs
        win_out_ref[...] = (win_ref[...] * scale_ref[...]).astype(win_out_ref.dtype)
    else:
        o_ref, wdst_ref, s_ref, t2_ref = refs
    _cast_slabs([wsrc_ref], [wdst_ref])
    lane = lax.broadcasted_iota(jnp.int32, (GRID_W, LANES), 1)
    lo = lane < NA_HEAD_DIM

    @pl.when(pl.program_id(0) == 0)
    def _():
        qcol = lax.broadcasted_iota(jnp.int32, (GRID_W, LANES), 0)
        kcol = jnp.where(lo, lane, lane - GRID_W)
        col_start = jnp.clip(qcol - WIN_W // 2, 0, GRID_W - WIN_W)
        off = kcol - col_start

        def build(ridx, carry):
            for h in range(NA_HEADS):
                xa = jnp.broadcast_to(rpb_ref[h, pl.ds(ridx, 1), :], (GRID_W, LANES))
                xb = jnp.broadcast_to(rpb_ref[h, pl.ds(ridx + 1, 1), :], (GRID_W, LANES))
                ta = pltpu.roll(xa, GRID_W + 1, 1, stride=1, stride_axis=0)
                tb = pltpu.roll(xb, 1, 1, stride=1, stride_axis=0)
                tile = jnp.where(off >= 0, jnp.where(off < WIN_W, jnp.where(lo, ta, tb), MASK_NEG), MASK_NEG)
                t2_ref[h // 2, ridx, (h % 2) * GRID_W:(h % 2 + 1) * GRID_W, :] = tile
            return carry

        lax.fori_loop(0, NA_BIAS_ROWS, build, 0)

    def window(r):
        if isinstance(r, int):
            r_start = min(max(r - WIN_H // 2, 0), GRID_H - WIN_H)
            return r_start * GRID_W, r_start - r + (WIN_H - 1)
        r_start = jnp.clip(r - WIN_H // 2, 0, GRID_H - WIN_H)
        return pl.multiple_of(r_start * GRID_W, GRID_W), r_start - r + (WIN_H - 1)

    def token_row(r):
        return r * GRID_W if isinstance(r, int) else pl.multiple_of(r * GRID_W, GRID_W)

    def scores(r, slot):
        kw0, d = window(r)
        q_r = q_ref[pl.ds(token_row(r), GRID_W), :]
        for p in range(NA_PAIRS):
            cols = slice(p * LANES, (p + 1) * LANES)
            qp = q_r[:, cols]
            zero = jnp.zeros_like(qp)
            q2 = jnp.concatenate([jnp.where(lo, qp, zero), jnp.where(lo, zero, qp)], axis=0)
            kp = k_ref[pl.ds(kw0, NA_KEYS), cols]
            s = lax.dot_general(q2, kp, (((1,), (1,)), ((), ())), preferred_element_type=F32)
            for c in range(NA_KEYS // LANES):
                kc = slice(c * LANES, (c + 1) * LANES)
                s_ref[slot, p, :, kc] = s[:, kc] + t2_ref[p, d + 2 * c]

    def attend(r, slot):
        kw0, _ = window(r)
        outs = []
        for p in range(NA_PAIRS):
            cols = slice(p * LANES, (p + 1) * LANES)
            s = s_ref[slot, p]
            e = jnp.exp2(s - jnp.max(s, axis=-1, keepdims=True))
            l = jnp.sum(e, axis=-1, keepdims=True)
            vp = v_ref[pl.ds(kw0, NA_KEYS), cols]
            o = jnp.dot(e.astype(BF16), vp, preferred_element_type=F32) / l
            outs.append(jnp.where(lo, o[:GRID_W], o[GRID_W:]))
        o_ref[pl.ds(token_row(r), GRID_W), :] = (
            jnp.concatenate(outs, axis=1).astype(o_ref.dtype))

    scores(0, 0)

    def rows(j, carry):
        r0 = NA_ROWS_PER_STEP * j
        for i in range(NA_ROWS_PER_STEP):
            scores(jnp.minimum(r0 + i + 1, GRID_H - 1), (i + 1) % 2)
            attend(r0 + i, i % 2)
        return carry

    lax.fori_loop(0, GRID_H // NA_ROWS_PER_STEP, rows, 0)


NA_RPB_ROWS = 2 * WIN_H
NA_RPB_LANE0 = GRID_W - WIN_W


def _na_pad_rpb(rpb):
    pad = ((0, 0), (0, 0), (0, NA_RPB_ROWS - rpb.shape[2]),
           (NA_RPB_LANE0, LANES - rpb.shape[3] - NA_RPB_LANE0))
    return jnp.pad(rpb.astype(F32) * LOG2E, pad)


def _na(proj, rpb, w_f32, w_in, in_scale, layer):
    blk = P_NA // NA_WIDTH
    cast_in, cast_out, cast_shape = _cast_specs([w_f32], layer)
    next_w_in = layer + 1 < DEPTH
    if next_w_in:
        win_in, win_out, win_shape = _cast_specs([w_in], layer + 1)
        cast_in = cast_in + win_in + [pl.BlockSpec((1, IN_COLS), lambda b: (0, 0))]
        cast_out, cast_shape = cast_out + win_out, cast_shape + win_shape
    return pl.pallas_call(
        functools.partial(_na_kernel, next_w_in=next_w_in),
        grid=(BATCH,),
        in_specs=[
            pl.BlockSpec((SEQ, NA_WIDTH), lambda b: (b, blk)),
            pl.BlockSpec((SEQ, NA_WIDTH), lambda b: (b, blk + 1)),
            pl.BlockSpec((SEQ, NA_WIDTH), lambda b: (b, blk + 2)),
            _layer_spec((NA_HEADS, NA_RPB_ROWS, LANES), lambda b: (layer, 0, 0, 0)),
        ] + cast_in,
        out_specs=[pl.BlockSpec((SEQ, NA_WIDTH), lambda b: (b, 0))] + cast_out,
        out_shape=[jax.ShapeDtypeStruct((TOKENS, NA_WIDTH), BF16)] + cast_shape,
        scratch_shapes=[pltpu.VMEM((2, NA_PAIRS, 2 * GRID_W, NA_KEYS), F32),
                        pltpu.VMEM((NA_PAIRS, NA_BIAS_ROWS, 2 * GRID_W, LANES), F32)],
        compiler_params=_cparams(("arbitrary",)),
        name="na",
    )(proj, proj, proj, rpb, w_f32, *((w_in, in_scale) if next_w_in else ()))


RET_NCHUNK = SEQ // RET_CHUNK


def _ret_kernel(q_ref, k_ref, v_ref, g_ref, dmat_ref, dec_ref, cdec_ref, gn_ref, o_ref, st_ref):
    def decayed(t, dec):
        return (t.astype(F32) * dec).astype(BF16)

    chunk = lambda c: slice(c * RET_CHUNK, (c + 1) * RET_CHUNK)
    last = RET_NCHUNK - 1

    for h in range(RET_HEADS):
        hd = slice(h * RET_HEAD_DIM, (h + 1) * RET_HEAD_DIM)
        dmat = dmat_ref[h]
        q_f, k_f, q_b, k_b = dec_ref[h, 0], dec_ref[h, 1], dec_ref[h, 2], dec_ref[h, 3]
        c_f, c_b = cdec_ref[h, 0:1, :], cdec_ref[h, 1:2, :]

        def summary(c, kdec):
            return lax.dot_general(decayed(k_ref[chunk(c), hd], kdec), v_ref[chunk(c), hd],
                                   (((0,), (0,)), ((), ())), preferred_element_type=F32)

        kv_f = [summary(c, k_f) for c in range(last)]
        kv_b = [None] + [summary(c, k_b) for c in range(1, RET_NCHUNK)]

        state = None
        for c in range(1, RET_NCHUNK):
            state = kv_f[c - 1] if state is None else state * c_f + kv_f[c - 1]
            st_ref[h, 0, c] = state.astype(BF16)
        state = None
        for c in range(last - 1, -1, -1):
            state = kv_b[c + 1] if state is None else state * c_b + kv_b[c + 1]
            st_ref[h, 1, c] = state.astype(BF16)

        gn = gn_ref[:, hd]
        for c in range(RET_NCHUNK):
            qc, kc, vc = q_ref[chunk(c), hd], k_ref[chunk(c), hd], v_ref[chunk(c), hd]
            s = lax.dot_general(qc, kc, (((1,), (1,)), ((), ())), preferred_element_type=F32) * dmat
            y = jnp.dot(s.astype(BF16), vc, preferred_element_type=F32)
            if c > 0:
                y = y + jnp.dot(decayed(qc, q_f), st_ref[h, 0, c], preferred_element_type=F32)
            if c < last:
                y = y + jnp.dot(decayed(qc, q_b), st_ref[h, 1, c], preferred_element_type=F32)
            y = y * _rms_scale(y) * gn
            o_ref[chunk(c), hd] = (g_ref[chunk(c), hd].astype(F32) * y).astype(o_ref.dtype)


def _ret_tables():
    pos = np.arange(SEQ, dtype=np.float64)
    theta = 1.0 / (RET_THETA_BASE ** np.linspace(0.0, 1.0, RET_HEAD_DIM // 2))
    ang = pos[:, None] * theta[None, :]
    sign = np.tile(np.array([-1.0, 1.0]), RET_HEAD_DIM // 2)
    cos = np.repeat(np.cos(ang), 2, axis=1)
    sin = np.repeat(np.sin(ang), 2, axis=1) * sign
    hidx = np.arange(RET_HEADS, dtype=np.float64)
    lg_f = np.log1p(-np.exp2(-5.0 - hidx))[:, None]
    lg_b = np.log1p(-np.exp2(-5.5 - hidx))[:, None]
    idx = np.arange(RET_CHUNK, dtype=np.float64)
    diff = idx[:, None] - idx[None, :]
    dmat = np.where(diff >= 0, np.exp(lg_f[:, :, None] * np.maximum(diff, 0.0)),
                    np.exp(lg_b[:, :, None] * np.maximum(-diff, 0.0)))
    dec = np.stack([np.exp(lg_f * (idx + 1.0)), np.exp(lg_f * (RET_CHUNK - 1.0 - idx)),
                    np.exp(lg_b * (RET_CHUNK - idx)), np.exp(lg_b * idx)], axis=1)
    dec = np.broadcast_to(dec[..., None], (RET_HEADS, 4, RET_CHUNK, LANES))
    cdec = np.stack([np.exp(lg_f * RET_CHUNK), np.exp(lg_b * RET_CHUNK)], axis=1)
    cdec = np.broadcast_to(cdec, (RET_HEADS, 2, LANES))
    return tuple(jnp.asarray(np.ascontiguousarray(t), F32) for t in (cos, sin, dmat, dec, cdec))


def _retention(proj, gn, tables, layer):
    _, _, dmat, dec, cdec = tables
    blk = P_RET // RET_WIDTH
    tok = lambda off: pl.BlockSpec((SEQ, RET_WIDTH), lambda b: (b, blk + off))
    return pl.pallas_call(
        _ret_kernel,
        grid=(BATCH,),
        in_specs=[
            tok(0), tok(1), tok(2), tok(3),
            pl.BlockSpec((RET_HEADS, RET_CHUNK, RET_CHUNK), lambda b: (0, 0, 0)),
            pl.BlockSpec((RET_HEADS, 4, RET_CHUNK, LANES), lambda b: (0, 0, 0, 0)),
            pl.BlockSpec((RET_HEADS, 2, LANES), lambda b: (0, 0, 0)),
            _layer_spec((1, RET_WIDTH), lambda b: (layer, 0, 0)),
        ],
        out_specs=pl.BlockSpec((SEQ, RET_WIDTH), lambda b: (b, 0)),
        out_shape=jax.ShapeDtypeStruct((TOKENS, RET_WIDTH), BF16),
        scratch_shapes=[pltpu.VMEM((RET_HEADS, 2, RET_NCHUNK, RET_HEAD_DIM, RET_HEAD_DIM), BF16)],
        compiler_params=_cparams(("parallel",)),
        name="retention",
    )(proj, proj, proj, proj, dmat, dec, cdec, gn)


LRU_SEGS = SUBLANES
LRU_SEG = SEQ // LRU_SEGS
LRU_PAD = SUBLANES
LRU_SLABS = LRU_WIDTH // LANES


def _softplus(x):
    return jnp.maximum(x, 0.0) + jnp.log1p(jnp.exp(-jnp.abs(x)))


LRU_NCAST = 5


def _lru_kernel(xc_ref, gc_ref, cw_ref, cb_ref, w_ref, bias_ref, lam_ref, *refs):
    o_ref = refs[LRU_NCAST]
    xpad_ref, af_ref, ab_ref, bf_ref, bb_ref = refs[2 * LRU_NCAST + 1:]
    _cast_slabs(refs[:LRU_NCAST], refs[LRU_NCAST + 1:2 * LRU_NCAST + 1])
    a_ref, b_ref = (af_ref, ab_ref), (bf_ref, bb_ref)
    zeros = jnp.zeros((LRU_PAD, LRU_WIDTH), F32)
    xpad_ref[0:LRU_PAD, :] = zeros
    xpad_ref[LRU_PAD + SEQ:, :] = zeros
    xpad_ref[LRU_PAD:LRU_PAD + SEQ, :] = xc_ref[...].astype(F32)

    cw = cw_ref[...]
    cb = cb_ref[...]
    half_k = (-0.5 * LRU_C * LOG2E) * _softplus(-lam_ref[...])
    row = lax.broadcasted_iota(jnp.int32, (LRU_SEG, LANES), 0)

    def gates(seg, first_dir):
        t0 = pl.multiple_of(seg * LRU_SEG, LRU_SEG)
        xw = xpad_ref[pl.ds(t0, LRU_SEG + 2 * LRU_PAD), :]
        taps = [xw[LRU_PAD - 2 + j:LRU_PAD - 2 + j + LRU_SEG, :] for j in range(4)]
        xf = taps[0] * cw[0:1] + taps[1] * cw[1:2] + taps[2] * cw[2:3] + taps[3] * cw[3:4] + cb
        for s in range(LRU_SLABS):
            cols = slice(s * LANES, (s + 1) * LANES)
            xs = xf[:, cols]
            xh = 0.5 * xs
            z = jnp.dot(xs.astype(BF16), w_ref[s], preferred_element_type=F32)
            for dr in range(2):
                zr = z[:, (2 * dr) * LANES:(2 * dr + 1) * LANES] + bias_ref[2 * dr:2 * dr + 1, cols]
                zi = z[:, (2 * dr + 1) * LANES:(2 * dr + 2) * LANES] + bias_ref[2 * dr + 1:2 * dr + 2, cols]
                hk = half_k[dr:dr + 1, cols]
                a = jnp.exp2(jnp.tanh(zr) * hk + hk)
                if first_dir == dr:
                    a = jnp.where(row == (0 if dr == 0 else LRU_SEG - 1), 0.0, a)
                om = 1.0 - a * a
                mult = om * lax.rsqrt(jnp.maximum(om, 1e-30))
                a_ref[dr][s, pl.ds(seg, LRU_SEG, stride=LRU_SEGS), :] = a
                b_ref[dr][s, pl.ds(seg, LRU_SEG, stride=LRU_SEGS), :] = mult * (jnp.tanh(zi) * xh + xh)

    gates(0, 0)
    lax.fori_loop(1, LRU_SEGS - 1, lambda seg, c: (gates(seg, None), c)[1], 0)
    gates(LRU_SEGS - 1, 1)

    def group(dr, j):
        return pl.ds(pl.multiple_of((j if dr == 0 else LRU_SEG - 1 - j) * LRU_SEGS, LRU_SEGS), LRU_SEGS)

    def probe(j, carry):
        new = []
        for dr in range(2):
            for s in range(LRU_SLABS):
                h, p = carry[dr * LRU_SLABS + s]
                a = a_ref[dr][s, group(dr, j), :]
                new.append((a * h + b_ref[dr][s, group(dr, j), :], a * p))
        return tuple(new)

    init = (jnp.zeros((LRU_SEGS, LANES), F32), jnp.ones((LRU_SEGS, LANES), F32))
    ends = lax.fori_loop(0, LRU_SEG, probe, (init,) * (2 * LRU_SLABS), unroll=2)

    def carry_in(dr, s):
        h_end, p_end = ends[dr * LRU_SLABS + s]
        state = jnp.zeros((1, LANES), F32)
        rows_in = [None] * LRU_SEGS
        for seg in (range(LRU_SEGS) if dr == 0 else range(LRU_SEGS - 1, -1, -1)):
            rows_in[seg] = state
            state = h_end[seg:seg + 1, :] + p_end[seg:seg + 1, :] * state
        return jnp.concatenate(rows_in, axis=0)

    def scan(j, hs):
        new = []
        for dr in range(2):
            for s in range(LRU_SLABS):
                h = a_ref[dr][s, group(dr, j), :] * hs[dr * LRU_SLABS + s] + b_ref[dr][s, group(dr, j), :]
                b_ref[dr][s, group(dr, j), :] = h
                new.append(h)
        return tuple(new)

    lax.fori_loop(0, LRU_SEG, scan, tuple(carry_in(dr, s) for dr in range(2) for s in range(LRU_SLABS)),
                  unroll=2)

    def finish(seg, carry):
        t0 = pl.multiple_of(seg * LRU_SEG, LRU_SEG)
        g = jax.nn.gelu(gc_ref[pl.ds(t0, LRU_SEG), :].astype(F32), approximate=True)
        for s in range(LRU_SLABS):
            cols = slice(s * LANES, (s + 1) * LANES)
            y = (b_ref[0][s, pl.ds(seg, LRU_SEG, stride=LRU_SEGS), :]
                 + b_ref[1][s, pl.ds(seg, LRU_SEG, stride=LRU_SEGS), :])
            o_ref[pl.ds(t0, LRU_SEG), cols] = (g[:, cols] * y).astype(o_ref.dtype)
        return carry

    lax.fori_loop(0, LRU_SEGS, finish, 0)


def _lru_weights(wa, wx):
    def slab(w):
        w = 0.5 * w.reshape(DEPTH, LRU_SLABS, 2, LRU_BLOCK_DIM, LRU_BLOCK_DIM)
        z = jnp.zeros_like(w[:, :, 0])
        top = jnp.concatenate([w[:, :, 0], z], axis=-1)
        bot = jnp.concatenate([z, w[:, :, 1]], axis=-1)
        return jnp.concatenate([top, bot], axis=-2)
    return jnp.concatenate([slab(wa[:, 0]), slab(wx[:, 0]), slab(wa[:, 1]), slab(wx[:, 1])],
                           axis=-1).astype(BF16)


def _rglru(proj, conv_w, conv_b, w4, bias4, lam, side_weights, layer):
    blk = P_LRU // LRU_WIDTH
    full = lambda *shape: _layer_spec(shape, lambda b: (layer,) + (0,) * len(shape))
    state = pltpu.VMEM((LRU_SLABS, SEQ, LANES), F32)
    cast_in, cast_out, cast_shape = _cast_specs(side_weights, layer)
    return pl.pallas_call(
        _lru_kernel,
        grid=(BATCH,),
        in_specs=[
            pl.BlockSpec((SEQ, LRU_WIDTH), lambda b: (b, blk)),
            pl.BlockSpec((SEQ, LRU_WIDTH), lambda b: (b, blk + 1)),
            full(4, LRU_WIDTH), full(1, LRU_WIDTH),
            full(LRU_SLABS, LANES, 4 * LANES), full(4, LRU_WIDTH), full(2, LRU_WIDTH),
        ] + cast_in,
        out_specs=[pl.BlockSpec((SEQ, LRU_WIDTH), lambda b: (b, 0))] + cast_out,
        out_shape=[jax.ShapeDtypeStruct((TOKENS, LRU_WIDTH), BF16)] + cast_shape,
        scratch_shapes=[pltpu.VMEM((SEQ + 2 * LRU_PAD, LRU_WIDTH), F32), state, state, state, state],
        compiler_params=_cparams(("parallel",)),
        name="rglru",
    )(proj, proj, conv_w, conv_b, w4, bias4, lam, *side_weights)


TAIL_TM = 512
MLP_TF = 512
GATE_HALF = D_MODEL // 2


def _merged_branches(y_refs, gate_refs, bg_ref, wb_ref):
    branches = tuple(zip(y_refs, gate_refs))
    halves = []
    for half in range(D_MODEL // GATE_HALF):
        cols = slice(half * GATE_HALF, (half + 1) * GATE_HALF)
        merged = None
        for n, (y_ref, gate_ref) in enumerate(branches):
            branch = jnp.dot(y_ref[...], wb_ref[n * NA_WIDTH:(n + 1) * NA_WIDTH, cols],
                             preferred_element_type=F32)
            gate = jax.nn.sigmoid(gate_ref[:, cols].astype(F32) + bg_ref[n:n + 1, cols])
            merged = gate * branch if merged is None else merged + gate * branch
        halves.append(merged.astype(BF16))
    return jnp.concatenate(halves, axis=1)


def _tail_kernel(x_ref, yn_ref, yr_ref, yl_ref, g0_ref, g1_ref, g2_ref, bg_ref, wb_ref, wo_ref,
                 g_ref, wu_ref, wd_ref, p_ref, gp_ref, wg_ref, wp_ref, gf_ref, o_ref, u_ref, *, final):
    merged = _merged_branches((yn_ref, yr_ref, yl_ref), (g0_ref, g1_ref, g2_ref), bg_ref, wb_ref)
    x = x_ref[...] + jnp.dot(merged, wo_ref[...], preferred_element_type=F32)
    h = (x * _rms_scale(x) * g_ref[...]).astype(BF16)
    for c in range(D_FF // MLP_TF):
        ff = slice(c * MLP_TF, (c + 1) * MLP_TF)
        u = jnp.maximum(jnp.dot(h, wu_ref[:, ff], preferred_element_type=F32), 0.0)
        u_ref[:, ff] = (u * u).astype(BF16)
    x = x + jnp.dot(u_ref[...], wd_ref[...], preferred_element_type=F32)
    h = (x * _rms_scale(x) * gp_ref[...]).astype(BF16)
    gate = jax.nn.sigmoid(jnp.dot(h, wg_ref[...], preferred_element_type=F32))
    emb = jnp.dot(p_ref[...].astype(BF16), wp_ref[...], preferred_element_type=F32)
    x = x + gate * emb
    if final:
        x = x * _rms_scale(x) * gf_ref[...]
    o_ref[...] = x


def _tail(x, y_na, y_ret, y_lru, proj, b_gate, w_branch, w_out, g_mlp, w_up, w_down, p, g_ple, w_gate, w_ple,
          g_final, layer):
    xtile = pl.BlockSpec((TAIL_TM, D_MODEL), lambda i: (i, 0))
    ytile = pl.BlockSpec((TAIL_TM, NA_WIDTH), lambda i: (i, 0))
    gate = lambda n: pl.BlockSpec((TAIL_TM, D_MODEL), lambda i: (i, P_GATE // D_MODEL + n))
    vec = _layer_spec((1, D_MODEL), lambda i: (layer, 0, 0))
    resident = lambda *shape: _resident_spec(shape)
    final = layer == DEPTH - 1
    return pl.pallas_call(
        functools.partial(_tail_kernel, final=final),
        grid=(TOKENS // TAIL_TM,),
        in_specs=[
            xtile, ytile, ytile, ytile, gate(0), gate(1), gate(2),
            _layer_spec((N_BRANCH, D_MODEL), lambda i: (layer, 0, 0)),
            resident(N_BRANCH * NA_WIDTH, D_MODEL),
            resident(D_MODEL, D_MODEL),
            vec,
            resident(D_MODEL, D_FF),
            resident(D_FF, D_MODEL),
            _layer_spec((TAIL_TM, PLE_DIM), lambda i: (layer, i, 0)),
            vec,
            resident(D_MODEL, D_MODEL),
            resident(PLE_DIM, D_MODEL),
            pl.BlockSpec((1, D_MODEL), lambda i: (0, 0)),
        ],
        out_specs=xtile,
        out_shape=jax.ShapeDtypeStruct((TOKENS, D_MODEL), F32),
        scratch_shapes=[pltpu.VMEM((TAIL_TM, D_FF), BF16)],
        compiler_params=_cparams(("parallel",)),
        name="tail_final" if final else "tail",
    )(x, y_na, y_ret, y_lru, proj, proj, proj, b_gate, w_branch, w_out,
      g_mlp, w_up, w_down, p, g_ple, w_gate, w_ple, g_final)


_IN_COL_SCALE = np.ones((IN_COLS,), np.float32)
_IN_COL_SCALE[W_NA:W_NA + NA_WIDTH] = NA_QSCALE


def kernel(x, p, g_mix, w_in, b_gate, na_rpb, ret_gn, conv_w, conv_b, lru_wa, lru_ba, lru_wx, lru_bx,
           lru_lambda, w_branch, w_out, g_mlp, w_up, w_down, g_ple, w_ple_gate, w_ple, g_final):
    rows = lambda v: v.reshape(DEPTH, 1, -1).astype(F32)
    in_scale = jnp.asarray(_IN_COL_SCALE).reshape(1, IN_COLS)
    w_in_b = (w_in[0] * in_scale).astype(BF16)
    side_weights = (w_down, w_branch.reshape(DEPTH, N_BRANCH * NA_WIDTH, D_MODEL), w_out, w_ple_gate, w_ple)
    g_mix_r, g_mlp_r, g_ple_r, gn_r, conv_b_r = rows(g_mix), rows(g_mlp), rows(g_ple), rows(ret_gn), rows(conv_b)
    rpb = _na_pad_rpb(na_rpb)
    lru_w4 = _lru_weights(lru_wa, lru_wx)
    lru_bias4 = 0.5 * jnp.stack([lru_ba[:, 0], lru_bx[:, 0], lru_ba[:, 1], lru_bx[:, 1]], axis=1).astype(F32)
    ret_tables = _ret_tables()
    pt = p.reshape(DEPTH, TOKENS, PLE_DIM)
    g_final_r = g_final.reshape(1, D_MODEL).astype(F32)

    xt = x.reshape(TOKENS, D_MODEL)
    for i in range(DEPTH):
        proj = _in_proj(xt, g_mix_r, w_in_b, ret_tables[0], ret_tables[1], i)
        y_na, w_up_b, *next_w_in = _na(proj, rpb, w_up, w_in, in_scale, i)
        y_ret = _retention(proj, gn_r, ret_tables, i)
        y_lru, w_down_b, w_branch_b, w_out_b, w_gate_b, w_ple_b = _rglru(
            proj, conv_w.astype(F32), conv_b_r, lru_w4, lru_bias4, lru_lambda.astype(F32), side_weights, i)
        xt = _tail(xt, y_na, y_ret, y_lru, proj, b_gate.astype(F32), w_branch_b, w_out_b, g_mlp_r, w_up_b,
                   w_down_b, pt, g_ple_r, w_gate_b, w_ple_b, g_final_r, i)
        if next_w_in:
            w_in_b = next_w_in[0]
    return xt.reshape(BATCH, SEQ, D_MODEL)
```
